```python
import math
import jax, jax.numpy as jnp
from jax import lax
import numpy as np

D_MODEL = 1024
BATCH = 8
SEQ = 2048
DEPTH = 1
DEC_BATCH = 8
DEC_SEQ = 64
PAST_LEN = 2048

CHUNK = 64
N_META = 16
MIX_W = D_MODEL
ATT_W = MIX_W // 2
HEAD_DIM = 64
N_HEADS = ATT_W // HEAD_DIM
KV_HEADS = 2
GQA_GROUP = N_HEADS // KV_HEADS
WINDOW = 128
N_BACK = WINDOW // CHUNK
BAND = (N_BACK + 1) * CHUNK
N_BUCKETS = 32
MAX_DISTANCE = 128
RET_W = MIX_W - ATT_W
RET_HEADS = 4
RET_DK = RET_W // RET_HEADS
RET_DV = RET_W // RET_HEADS
ROPE_BASE = 10000.0
PROJ_SPLITS = (ATT_W, KV_HEADS * HEAD_DIM, KV_HEADS * HEAD_DIM, RET_W, RET_W, RET_W, RET_W)
PROJ_W = sum(PROJ_SPLITS)
N_KEYS = 128
N_EXPERTS = N_KEYS * N_KEYS
PEER_HEADS = 8
PEER_TOPK = 16
PEER_DK = 128
PEER_DK_HALF = PEER_DK // 2
PEER_BLOCK = 256
ALPHA = (2.0 * DEPTH) ** 0.25
BETA = (8.0 * DEPTH) ** -0.25
LN_EPS = 1e-5
NEG_INF = -1e30

kernel_name = 'hymba_swa_retention_peer_stream_step'


def layer_norm(x, g, b):
    xf = x.astype(jnp.float32)
    mu = xf.mean(-1, keepdims=True)
    var = jnp.square(xf - mu).mean(-1, keepdims=True)
    y = (xf - mu) * lax.rsqrt(var + LN_EPS) * g.astype(jnp.float32) + b.astype(jnp.float32)
    return y.astype(x.dtype)


def t5_bucket(rel):
    nb = N_BUCKETS // 2
    max_exact = nb // 2
    n = jnp.abs(rel)
    large = max_exact + (jnp.log(jnp.maximum(n, max_exact).astype(jnp.float32) / max_exact)
                         / math.log(MAX_DISTANCE / max_exact) * (nb - max_exact)).astype(jnp.int32)
    large = jnp.minimum(large, nb - 1)
    return jnp.where(rel > 0, nb, 0) + jnp.where(n < max_exact, n, large)


def rel_bias_lookup(rel_bias, rel_np):
    return rel_bias[t5_bucket(jnp.asarray(rel_np, jnp.int32))]


def rotary(x, pos):
    half = x.shape[-1] // 2
    inv = ROPE_BASE ** (-jnp.arange(half, dtype=jnp.float32) / half)
    ang = jnp.asarray(pos, jnp.float32)[:, None] * inv[None, :]
    cos = jnp.cos(ang)[None, :, None, :]
    sin = jnp.sin(ang)[None, :, None, :]
    x1, x2 = x[..., :half], x[..., half:]
    return jnp.concatenate([x1 * cos - x2 * sin, x1 * sin + x2 * cos], axis=-1)


def project(h, w, pos):
    p = jnp.einsum('bsd,de->bse', h, w)
    q, k, v, rq, rk, rv, rg = jnp.split(p, np.cumsum(PROJ_SPLITS)[:-1].tolist(), axis=-1)
    B, L = h.shape[:2]
    q = q.reshape(B, L, KV_HEADS, GQA_GROUP, HEAD_DIM)
    k = k.reshape(B, L, KV_HEADS, HEAD_DIM)
    v = v.reshape(B, L, KV_HEADS, HEAD_DIM)
    rq = rotary(rq.reshape(B, L, RET_HEADS, RET_DK).astype(jnp.float32), pos)
    rk = rotary(rk.reshape(B, L, RET_HEADS, RET_DK).astype(jnp.float32), pos) * (RET_DK ** -0.5)
    rv = rv.reshape(B, L, RET_HEADS, RET_DV).astype(jnp.float32)
    return q, k, v, rq, rk, rv, rg


def sink_attention(q, k, v, bias, mask, sinks):
    s = jnp.einsum('bnqhgd,bnkhd->bnhgqk', q, k, preferred_element_type=jnp.float32) * (HEAD_DIM ** -0.5)
    nb, nq, nk = bias.shape[:3]
    s = s + bias.astype(jnp.float32).reshape(nb, nq, nk, KV_HEADS, GQA_GROUP).transpose(0, 3, 4, 1, 2)[None]
    s = jnp.where(mask[None, :, None, None], s, NEG_INF)
    sink = sinks.astype(jnp.float32).reshape(KV_HEADS, GQA_GROUP)[None, None, :, :, None, None]
    m = jnp.maximum(s.max(-1, keepdims=True), sink)
    p = jnp.exp(s - m)
    denom = p.sum(-1, keepdims=True) + jnp.exp(sink - m)
    return jnp.einsum('bnhgqk,bnkhd->bnqhgd', p / denom, v.astype(jnp.float32))


def prompt_window_attention(q, k, v, mk, mv, rel_bias, sinks):
    B, S = q.shape[:2]
    nC = S // CHUNK
    qb = q.reshape(B, nC, CHUNK, KV_HEADS, GQA_GROUP, HEAD_DIM)
    pad = ((0, 0), (N_BACK * CHUNK, 0), (0, 0), (0, 0))
    kp = jnp.pad(k, pad).reshape(B, nC + N_BACK, CHUNK, KV_HEADS, HEAD_DIM)
    vp = jnp.pad(v, pad).reshape(B, nC + N_BACK, CHUNK, KV_HEADS, HEAD_DIM)
    meta_shape = (B, nC, N_META, KV_HEADS, HEAD_DIM)
    kb = jnp.concatenate([jnp.broadcast_to(mk[:, None], meta_shape)]
                         + [kp[:, j:j + nC] for j in range(N_BACK + 1)], axis=2)
    vb = jnp.concatenate([jnp.broadcast_to(mv[:, None], meta_shape)]
                         + [vp[:, j:j + nC] for j in range(N_BACK + 1)], axis=2)
    i = np.arange(CHUNK)
    jb = np.arange(BAND)
    c = np.arange(nC)
    m = np.arange(N_META)
    rel_meta = m[None, None, :] - (N_META + c[:, None, None] * CHUNK + i[None, :, None])
    rel_band = np.broadcast_to(jb[None, None, :] - N_BACK * CHUNK - i[None, :, None], (nC, CHUNK, BAND))
    rel = np.concatenate([rel_meta, rel_band], axis=-1)
    band_ok = np.broadcast_to((c[:, None, None] - N_BACK) * CHUNK + jb[None, None, :] >= 0, (nC, CHUNK, BAND))
    mask = np.concatenate([np.ones((nC, CHUNK, N_META), dtype=bool), band_ok], axis=-1)
    o = sink_attention(qb, kb, vb, rel_bias_lookup(rel_bias, rel), jnp.asarray(mask), sinks)
    return o.reshape(B, S, ATT_W)


def retention_chunk(state, q, k, v, log_gamma):
    L = q.shape[1]
    idx = jnp.arange(L, dtype=jnp.float32)
    diff = idx[:, None] - idx[None, :]
    decay = jnp.where(diff >= 0, jnp.exp(jnp.maximum(diff, 0.0)[None] * log_gamma[:, None, None]), 0.0)
    s = jnp.einsum('bqhd,bkhd->bhqk', q, k) * decay[None]
    inner = jnp.einsum('bhqk,bkhe->bqhe', s, v)
    q_dec = jnp.exp((idx + 1.0)[:, None] * log_gamma[None, :])
    cross = jnp.einsum('bqhd,bhde->bqhe', q, state) * q_dec[None, :, :, None]
    k_dec = jnp.exp((L - 1.0 - idx)[:, None] * log_gamma[None, :])
    new_state = (jnp.exp(L * log_gamma)[None, :, None, None] * state
                 + jnp.einsum('bkhd,bkhe->bhde', k * k_dec[None, :, :, None], v))
    return inner + cross, new_state


def prompt_retention(rq_m, rk_m, rv_m, rq, rk, rv, log_gamma):
    B, S = rq.shape[:2]
    nC = S // CHUNK
    st0 = jnp.zeros((B, RET_HEADS, RET_DK, RET_DV), jnp.float32)
    o_m, st = retention_chunk(st0, rq_m, rk_m, rv_m, log_gamma)

    def to_chunks(t):
        return t.reshape(B, nC, CHUNK, RET_HEADS, t.shape[-1]).swapaxes(0, 1)

    def step(s, qkv):
        o, s = retention_chunk(s, qkv[0], qkv[1], qkv[2], log_gamma)
        return s, o

    st, o = lax.scan(step, st, (to_chunks(rq), to_chunks(rk), to_chunks(rv)))
    o = o.swapaxes(0, 1).reshape(B, S, RET_HEADS, RET_DV)
    return o_m, o, st


def mix_out(o_att, o_ret, rg, gn_g, w_out):
    mu = o_ret.mean(-1, keepdims=True)
    var = jnp.square(o_ret - mu).mean(-1, keepdims=True)
    y = ((o_ret - mu) * lax.rsqrt(var + LN_EPS)).reshape(o_ret.shape[0], o_ret.shape[1], RET_W)
    y = y * gn_g.astype(jnp.float32) * jax.nn.silu(rg.astype(jnp.float32))
    mixed = jnp.concatenate([o_att, y], axis=-1).astype(rg.dtype)
    return jnp.einsum('bse,ed->bsd', mixed, w_out)


def peer_ffn(x, wq, sub_keys, u_tab, v_tab):
    lead = x.shape[:-1]
    xt = x.reshape(-1, D_MODEL)
    T = xt.shape[0]
    pad = (-T) % PEER_BLOCK
    xt = jnp.pad(xt, ((0, pad), (0, 0))).reshape(-1, PEER_BLOCK, D_MODEL)

    def block(xb):
        q = jnp.einsum('td,de->te', xb, wq).reshape(PEER_BLOCK, PEER_HEADS, 2, PEER_DK_HALF)
        s = jnp.einsum('tpcd,pcnd->tpcn', q, sub_keys, preferred_element_type=jnp.float32)
        s1, i1 = lax.top_k(s[:, :, 0], PEER_TOPK)
        s2, i2 = lax.top_k(s[:, :, 1], PEER_TOPK)
        cand_s = (s1[..., :, None] + s2[..., None, :]).reshape(PEER_BLOCK, PEER_HEADS, PEER_TOPK * PEER_TOPK)
        cand_i = (i1[..., :, None] * N_KEYS + i2[..., None, :]).reshape(PEER_BLOCK, PEER_HEADS, PEER_TOPK * PEER_TOPK)
        top_s, sel = lax.top_k(cand_s, PEER_TOPK)
        eidx = jnp.take_along_axis(cand_i, sel, axis=-1)
        g = jax.nn.softmax(top_s, axis=-1)
        act = jax.nn.gelu(jnp.einsum('tpkd,td->tpk', u_tab[eidx], xb, preferred_element_type=jnp.float32))
        w = (g * act).astype(xb.dtype)
        return jnp.einsum('tpk,tpkd->td', w, v_tab[eidx])

    out = lax.map(block, xt).reshape(-1, D_MODEL)[:T]
    return out.reshape(*lead, D_MODEL)


def post_layer(h, mix, ln1_g, ln1_b, wq, sub_keys, u_tab, v_tab, ln2_g, ln2_b):
    h = layer_norm(ALPHA * h + mix, ln1_g, ln1_b)
    return layer_norm(ALPHA * h + peer_ffn(h, wq, sub_keys, u_tab, v_tab), ln2_g, ln2_b)


def setup_inputs(seed: int = 0) -> dict:
    key = jax.random.key(seed)
    ks = jax.random.split(key, 32)
    f32 = jnp.float32
    nrm = lambda k, shape, scale=1.0: scale * jax.random.normal(k, shape, f32)
    swa_cache = min(WINDOW, PAST_LEN)
    return {
        'x_prompt': nrm(ks[0], (BATCH, SEQ, D_MODEL)),
        'x_sample': nrm(ks[1], (DEC_BATCH, DEC_SEQ, D_MODEL)),
        'cache_meta_k': nrm(ks[2], (DEPTH, DEC_BATCH, N_META, KV_HEADS, HEAD_DIM)),
        'cache_meta_v': nrm(ks[3], (DEPTH, DEC_BATCH, N_META, KV_HEADS, HEAD_DIM)),
        'cache_swa_k': nrm(ks[4], (DEPTH, DEC_BATCH, swa_cache, KV_HEADS, HEAD_DIM)),
        'cache_swa_v': nrm(ks[5], (DEPTH, DEC_BATCH, swa_cache, KV_HEADS, HEAD_DIM)),
        'state_ret': nrm(ks[6], (DEPTH, DEC_BATCH, RET_HEADS, RET_DK, RET_DV), 0.1),
        'meta_tokens': nrm(ks[7], (N_META, D_MODEL)),
        'ln_in_g': 1.0 + nrm(ks[8], (D_MODEL,), 0.02),
        'ln_in_b': nrm(ks[9], (D_MODEL,), 0.02),
        'rel_bias': nrm(ks[10], (N_BUCKETS, N_HEADS), 0.5),
        'w_in': nrm(ks[11], (DEPTH, D_MODEL, PROJ_W), D_MODEL ** -0.5),
        'w_out': nrm(ks[12], (DEPTH, MIX_W, D_MODEL), BETA * MIX_W ** -0.5),
        'attn_sinks': nrm(ks[13], (DEPTH, N_HEADS)),
        'ret_gn_g': 1.0 + nrm(ks[14], (DEPTH, RET_W), 0.02),
        'ln1_g': 1.0 + nrm(ks[15], (DEPTH, D_MODEL), 0.02),
        'ln1_b': nrm(ks[16], (DEPTH, D_MODEL), 0.02),
        'peer_wq': nrm(ks[17], (DEPTH, D_MODEL, PEER_HEADS * PEER_DK), D_MODEL ** -0.5),
        'peer_subkeys': nrm(ks[18], (DEPTH, PEER_HEADS, 2, N_KEYS, PEER_DK_HALF), PEER_DK_HALF ** -0.5),
        'peer_u': nrm(ks[19], (DEPTH, N_EXPERTS, D_MODEL), D_MODEL ** -0.5),
        'peer_v': nrm(ks[20], (DEPTH, N_EXPERTS, D_MODEL), BETA * PEER_HEADS ** -0.5),
        'ln2_g': 1.0 + nrm(ks[21], (DEPTH, D_MODEL), 0.02),
        'ln2_b': nrm(ks[22], (DEPTH, D_MODEL), 0.02),
    }


def reference(x_prompt, x_sample, cache_meta_k, cache_meta_v, cache_swa_k, cache_swa_v, state_ret,
              meta_tokens, ln_in_g, ln_in_b, rel_bias, w_in, w_out, attn_sinks, ret_gn_g, ln1_g, ln1_b,
              peer_wq, peer_subkeys, peer_u, peer_v, ln2_g, ln2_b):
    log_gamma = jnp.log(1.0 - 2.0 ** (-5.0 - jnp.arange(RET_HEADS, dtype=jnp.float32)))
    B, S, _ = x_prompt.shape
    DB, DS, _ = x_sample.shape
    pos_meta = np.arange(N_META)
    pos_p = N_META + np.arange(S)
    pos_s = N_META + PAST_LEN + np.arange(DS)

    h_meta = layer_norm(jnp.broadcast_to(meta_tokens.astype(x_prompt.dtype), (B, N_META, D_MODEL)), ln_in_g, ln_in_b)
    h = layer_norm(x_prompt, ln_in_g, ln_in_b)
    meta_k_l, meta_v_l, swa_k_l, swa_v_l, ret_l = [], [], [], [], []
    for l in range(DEPTH):
        ffn_p = (ln1_g[l], ln1_b[l], peer_wq[l], peer_subkeys[l], peer_u[l], peer_v[l], ln2_g[l], ln2_b[l])
        qm, km, vm, rqm, rkm, rvm, rgm = project(h_meta, w_in[l], pos_meta)
        q, k, v, rq, rk, rv, rg = project(h, w_in[l], pos_p)
        o_att = prompt_window_attention(q, k, v, km, vm, rel_bias, attn_sinks[l])
        o_ret_m, o_ret, st = prompt_retention(rqm, rkm, rvm, rq, rk, rv, log_gamma)
        h_next = post_layer(h, mix_out(o_att, o_ret, rg, ret_gn_g[l], w_out[l]), *ffn_p)
        if l < DEPTH - 1:
            bias_m = rel_bias_lookup(rel_bias, (pos_meta[None, :] - pos_meta[:, None])[None])
            o_att_m = sink_attention(qm[:, None], km[:, None], vm[:, None], bias_m,
                                     jnp.ones((1, N_META, N_META), dtype=bool), attn_sinks[l]).reshape(B, N_META, ATT_W)
            h_meta = post_layer(h_meta, mix_out(o_att_m, o_ret_m, rgm, ret_gn_g[l], w_out[l]), *ffn_p)
        meta_k_l.append(km)
        meta_v_l.append(vm)
        swa_k_l.append(k[:, S - WINDOW:])
        swa_v_l.append(v[:, S - WINDOW:])
        ret_l.append(st)
        h = h_next
    y_prompt = h

    hs = layer_norm(x_sample, ln_in_g, ln_in_b)
    ks_l, vs_l, rs_l = [], [], []
    W = cache_swa_k.shape[2]
    i = np.arange(DS)
    rel_s = np.concatenate([pos_meta[None, :] - pos_s[:, None],
                            np.arange(W)[None, :] - W - i[:, None],
                            i[None, :] - i[:, None]], axis=-1)[None]
    bias_s = rel_bias_lookup(rel_bias, rel_s)
    mask_s = jnp.ones(rel_s.shape, dtype=bool)
    for l in range(DEPTH):
        ffn_p = (ln1_g[l], ln1_b[l], peer_wq[l], peer_subkeys[l], peer_u[l], peer_v[l], ln2_g[l], ln2_b[l])
        q, k, v, rq, rk, rv, rg = project(hs, w_in[l], pos_s)
        kc = jnp.concatenate([cache_meta_k[l].astype(k.dtype), cache_swa_k[l].astype(k.dtype), k], axis=1)[:, None]
        vc = jnp.concatenate([cache_meta_v[l].astype(v.dtype), cache_swa_v[l].astype(v.dtype), v], axis=1)[:, None]
        o_att = sink_attention(q[:, None], kc, vc, bias_s, mask_s, attn_sinks[l]).reshape(DB, DS, ATT_W)
        o_ret, st_new = retention_chunk(state_ret[l].astype(jnp.float32), rq, rk, rv, log_gamma)
        hs = post_layer(hs, mix_out(o_att, o_ret, rg, ret_gn_g[l], w_out[l]), *ffn_p)
        ks_l.append(k)
        vs_l.append(v)
        rs_l.append(st_new)
    y_sample = hs

    return (y_prompt, y_sample,
            jnp.stack(meta_k_l), jnp.stack(meta_v_l), jnp.stack(swa_k_l), jnp.stack(swa_v_l), jnp.stack(ret_l),
            jnp.stack(ks_l), jnp.stack(vs_l), jnp.stack(rs_l))
```

```python
import functools
import math

import jax
import jax.numpy as jnp
import numpy as np
from jax import lax
from jax.experimental import pallas as pl
from jax.experimental.pallas import tpu as pltpu

D_MODEL = 1024
CHUNK = 64
N_META = 16
PAST_LEN = 2048
ATT_W = 512
HEAD_DIM = 64
N_HEADS = 8
KV_HEADS = 2
KV_W = KV_HEADS * HEAD_DIM
WINDOW = 128
N_BACK = WINDOW // CHUNK
BAND = (N_BACK + 1) * CHUNK
N_KEYS_ATT = BAND + N_META
N_BUCKETS = 32
MAX_DISTANCE = 128
RET_W = 512
RET_HEADS = 4
RET_DK = 128
ROPE_BASE = 10000.0
PROJ_W = 2816
N_KEYS = 128
N_EXPERTS = N_KEYS * N_KEYS
PEER_HEADS = 8
PEER_TOPK = 16
PEER_DK_HALF = 64
N_SEL = PEER_HEADS * PEER_TOPK
ALPHA = 2.0 ** 0.25
LN_EPS = 1e-5
NEG_INF = -1e30

LANES = 128
SUBLANES = 8
ROW_TILE = (SUBLANES, LANES)

PEER_TB = 8
PEER_SLOTS = 3
ROUTE_TM = 256
PROJ_TM = 256

_F32 = jnp.float32
_BF16 = jnp.bfloat16


def _vmem_limit(nbytes):
    return pltpu.CompilerParams(vmem_limit_bytes=int(nbytes))


def _layer_norm(x, g, b):
    mu = jnp.mean(x, axis=-1, keepdims=True)
    xc = x - mu
    var = jnp.mean(xc * xc, axis=-1, keepdims=True)
    return xc * lax.rsqrt(var + LN_EPS) * g + b


def _ln_proj_kernel(x_ref, g_ref, b_ref, w_ref, cos_ref, sin_ref,
                    q_ref, kv_ref, rq_ref, rk_ref, rv_ref, rg_ref):
    h = _layer_norm(x_ref[...], g_ref[...], b_ref[...])
    p = jnp.dot(h.astype(_BF16), w_ref[...], preferred_element_type=_F32)
    cos = cos_ref[...]
    sin = sin_ref[...]

    def rotary(a):
        outs = []
        for hh in range(RET_HEADS):
            seg = a[:, hh * RET_DK:(hh + 1) * RET_DK]
            outs.append(seg * cos + pltpu.roll(seg, RET_DK // 2, axis=1) * sin)
        return jnp.concatenate(outs, axis=-1)

    q_ref[...] = p[:, 0:512]
    kv_ref[...] = p[:, 512:768]
    rq_ref[...] = rotary(p[:, 768:1280])
    rk_ref[...] = rotary(p[:, 1280:1792]) * (RET_DK ** -0.5)
    rv_ref[...] = p[:, 1792:2304]
    rg_ref[...] = p[:, 2304:2816]


def _ln_proj(x2d, seq_len, ln_g, ln_b, w_in_bf, cos, sin, tm):
    t = x2d.shape[0]
    nblk_s = seq_len // tm
    row = lambda i: (i, 0)
    const = lambda i: (0, 0)
    outs = [jax.ShapeDtypeStruct((t, w), _F32) for w in (512, 256, 512, 512, 512, 512)]
    return pl.pallas_call(
        _ln_proj_kernel,
        grid=(t // tm,),
        in_specs=[
            pl.BlockSpec((tm, D_MODEL), row),
            pl.BlockSpec((1, D_MODEL), const),
            pl.BlockSpec((1, D_MODEL), const),
            pl.BlockSpec((D_MODEL, PROJ_W), const),
            pl.BlockSpec((tm, RET_DK), lambda i: (i % nblk_s, 0)),
            pl.BlockSpec((tm, RET_DK), lambda i: (i % nblk_s, 0)),
        ],
        out_specs=[pl.BlockSpec((tm, w), row) for w in (512, 256, 512, 512, 512, 512)],
        out_shape=outs,
        compiler_params=_vmem_limit(48 * 2 ** 20),
        name="ln_proj",
    )(x2d, ln_g, ln_b, w_in_bf, cos, sin)


def _t5_bucket_np(rel):
    nb = N_BUCKETS // 2
    max_exact = nb // 2
    n = np.abs(rel)
    large = max_exact + (np.log(np.maximum(n, max_exact).astype(np.float32) / max_exact)
                         / math.log(MAX_DISTANCE / max_exact) * (nb - max_exact)).astype(np.int32)
    large = np.minimum(large, nb - 1)
    return (np.where(rel > 0, nb, 0) + np.where(n < max_exact, n, large)).astype(np.int32)


def _bias_buckets():
    i = np.arange(CHUNK)
    jb = np.arange(BAND)
    m = np.arange(N_META)
    rel_band = jb[None, :] - N_BACK * CHUNK - i[:, None]
    out = []
    for c in range(3):
        rel_meta = m[None, :] - (N_META + c * CHUNK + i[:, None])
        out.append(np.concatenate([rel_band, rel_meta], axis=-1))
    return _t5_bucket_np(np.stack(out))


def _rel_bias_kernel(rb_ref, bucket_ref, o_ref):
    h = pl.program_id(1)
    bucket = bucket_ref[0]
    acc = jnp.zeros(bucket.shape, _F32)
    for v in range(N_BUCKETS):
        acc = jnp.where(bucket == v, rb_ref[v, h], acc)
    o_ref[0, 0] = acc


def _rel_bias_tables(rel_bias):
    buckets = jnp.asarray(_bias_buckets())
    return pl.pallas_call(
        _rel_bias_kernel,
        grid=(3, N_HEADS),
        in_specs=[
            pl.BlockSpec(memory_space=pltpu.SMEM),
            pl.BlockSpec((1, CHUNK, N_KEYS_ATT), lambda v, h: (v, 0, 0)),
        ],
        out_specs=pl.BlockSpec((1, 1, CHUNK, N_KEYS_ATT), lambda v, h: (v, h, 0, 0)),
        out_shape=jax.ShapeDtypeStruct((3, N_HEADS, CHUNK, N_KEYS_ATT), _F32),
        name="rel_bias",
    )(rel_bias, buckets)


def _attend(q, kk_band, vv_band, kk_meta, vv_meta, bias_ref, sinks_ref, min_valid_col):
    lane = lax.broadcasted_iota(jnp.int32, (1, LANES), 1)
    lo = lane < HEAD_DIM
    kk = jnp.concatenate([kk_band, kk_meta], axis=0)
    vv = jnp.concatenate([vv_band, vv_meta], axis=0)
    kk_r = pltpu.roll(kk, HEAD_DIM, axis=1)
    vv_r = pltpu.roll(vv, HEAD_DIM, axis=1)
    k_dup = [jnp.where(lo, kk, kk_r).astype(_BF16), jnp.where(lo, kk_r, kk).astype(_BF16)]
    v_lo = [jnp.where(lo, vv, 0.0).astype(_BF16), jnp.where(lo, vv_r, 0.0).astype(_BF16)]
    v_hi = [jnp.where(lo, 0.0, vv_r).astype(_BF16), jnp.where(lo, 0.0, vv).astype(_BF16)]
    col = lax.broadcasted_iota(jnp.int32, (CHUNK, N_KEYS_ATT), 1)
    valid = col >= min_valid_col
    outs = []
    for pair in range(N_HEADS // 2):
        kvh = pair // 2
        q2 = q[:, pair * LANES:(pair + 1) * LANES]
        o_pair = jnp.zeros((CHUNK, LANES), _F32)
        for half in range(2):
            head = 2 * pair + half
            qm = (jnp.where(lo, q2, 0.0) if half == 0 else jnp.where(lo, 0.0, q2)).astype(_BF16)
            s = lax.dot_general(qm, k_dup[kvh], (((1,), (1,)), ((), ())),
                                preferred_element_type=_F32)
            s = s * (HEAD_DIM ** -0.5) + bias_ref[head]
            s = jnp.where(valid, s, NEG_INF)
            sink = sinks_ref[head]
            m = jnp.maximum(jnp.max(s, axis=-1, keepdims=True), sink)
            p = jnp.exp(s - m)
            denom = jnp.sum(p, axis=-1, keepdims=True) + jnp.exp(sink - m)
            pn = (p / denom).astype(_BF16)
            vsel = v_lo[kvh] if half == 0 else v_hi[kvh]
            o_pair = o_pair + jnp.dot(pn, vsel, preferred_element_type=_F32)
        outs.append(o_pair)
    return jnp.concatenate(outs, axis=-1)


def _attn_prompt_kernel(sinks_ref, q_ref, kv2_ref, kv1_ref, kv0_ref, meta_ref, bias_ref, o_ref):
    c = pl.program_id(1)
    kvs = [kv2_ref[0], kv1_ref[0], kv0_ref[0]]
    kk_band = jnp.concatenate([kv[:, :KV_W] for kv in kvs], axis=0)
    vv_band = jnp.concatenate([kv[:, KV_W:] for kv in kvs], axis=0)
    meta = meta_ref[...]
    min_valid = (N_BACK - jnp.minimum(c, N_BACK)) * CHUNK
    o_ref[0] = _attend(q_ref[0], kk_band, vv_band, meta[:, :KV_W], meta[:, KV_W:],
                       bias_ref.at[0], sinks_ref, min_valid)


def _attention_prompt(q, kv, kv_meta, bias_tab, sinks):
    b, s, _ = q.shape
    nc = s // CHUNK

    def back(j):
        return lambda bi, c: (bi, jnp.maximum(c - j, 0), 0)

    return pl.pallas_call(
        _attn_prompt_kernel,
        grid=(b, nc),
        in_specs=[
            pl.BlockSpec(memory_space=pltpu.SMEM),
            pl.BlockSpec((1, CHUNK, ATT_W), lambda bi, c: (bi, c, 0)),
            pl.BlockSpec((1, CHUNK, 2 * KV_W), back(2)),
            pl.BlockSpec((1, CHUNK, 2 * KV_W), back(1)),
            pl.BlockSpec((1, CHUNK, 2 * KV_W), back(0)),
            pl.BlockSpec((N_META, 2 * KV_W), lambda bi, c: (0, 0)),
            pl.BlockSpec((1, N_HEADS, CHUNK, N_KEYS_ATT),
                         lambda bi, c: (jnp.minimum(c, N_BACK), 0, 0, 0)),
        ],
        out_specs=pl.BlockSpec((1, CHUNK, ATT_W), lambda bi, c: (bi, c, 0)),
        out_shape=jax.ShapeDtypeStruct((b, s, ATT_W), _F32),
        name="attn_prompt",
    )(sinks, q, kv, kv, kv, kv_meta, bias_tab)


def _attn_sample_kernel(sinks_ref, q_ref, kv_ref, ck_ref, cv_ref, mk_ref, mv_ref, bias_ref, o_ref):
    kv = kv_ref[0]
    kk_band = jnp.concatenate([ck_ref[0], kv[:, :KV_W]], axis=0)
    vv_band = jnp.concatenate([cv_ref[0], kv[:, KV_W:]], axis=0)
    o_ref[0] = _attend(q_ref[0], kk_band, vv_band, mk_ref[0], mv_ref[0],
                       bias_ref.at[0], sinks_ref, 0)


def _attention_sample(q, kv, cache_k, cache_v, meta_k, meta_v, bias_tab, sinks):
    b = q.shape[0]
    per_b = lambda bi: (bi, 0, 0)
    return pl.pallas_call(
        _attn_sample_kernel,
        grid=(b,),
        in_specs=[
            pl.BlockSpec(memory_space=pltpu.SMEM),
            pl.BlockSpec((1, CHUNK, ATT_W), per_b),
            pl.BlockSpec((1, CHUNK, 2 * KV_W), per_b),
            pl.BlockSpec((1, WINDOW, KV_W), per_b),
            pl.BlockSpec((1, WINDOW, KV_W), per_b),
            pl.BlockSpec((1, N_META, KV_W), per_b),
            pl.BlockSpec((1, N_META, KV_W), per_b),
            pl.BlockSpec((1, N_HEADS, CHUNK, N_KEYS_ATT), lambda bi: (N_BACK, 0, 0, 0)),
        ],
        out_specs=pl.BlockSpec((1, CHUNK, ATT_W), per_b),
        out_shape=jax.ShapeDtypeStruct((b, CHUNK, ATT_W), _F32),
        name="attn_sample",
    )(sinks, q, kv, cache_k, cache_v, meta_k, meta_v, bias_tab)


def _retention_kernel(gl_ref, st0_ref, rq_ref, rk_ref, rv_ref, decay_ref, qdec_ref, kdec_ref,
                      o_ref, st_ref):
    @pl.when(pl.program_id(0) == 0)
    def _():
        st_ref[...] = st0_ref[...]

    nb = rq_ref.shape[0]
    nt_dims = (((1,), (1,)), ((), ()))
    tn_dims = (((0,), (0,)), ((), ()))
    for b in range(nb):
        outs = []
        for h in range(RET_HEADS):
            sl = slice(h * RET_DK, (h + 1) * RET_DK)
            q = rq_ref[b, :, sl]
            k = rk_ref[b, :, sl]
            v = rv_ref[b, :, sl]
            qb = q.astype(_BF16)
            vb = v.astype(_BF16)
            s = lax.dot_general(qb, k.astype(_BF16), nt_dims, preferred_element_type=_F32)
            s = s * decay_ref[h]
            inner = jnp.dot(s.astype(_BF16), vb, preferred_element_type=_F32)
            st = st_ref[b, h]
            cross = jnp.dot(qb, st.astype(_BF16), preferred_element_type=_F32) * qdec_ref[h]
            outs.append(inner + cross)
            kd = (k * kdec_ref[h]).astype(_BF16)
            upd = lax.dot_general(kd, vb, tn_dims, preferred_element_type=_F32)
            st_ref[b, h] = gl_ref[h] * st + upd
        o_ref[b] = jnp.concatenate(outs, axis=-1)


def _retention_tables(chunk_len):
    log_gamma = jnp.log(1.0 - 2.0 ** (-5.0 - jnp.arange(RET_HEADS, dtype=_F32)))
    idx = jnp.arange(chunk_len, dtype=_F32)
    diff = idx[:, None] - idx[None, :]
    decay = jnp.where(diff >= 0, jnp.exp(jnp.maximum(diff, 0.0)[None] * log_gamma[:, None, None]), 0.0)
    q_dec = jnp.exp((idx + 1.0)[None, :] * log_gamma[:, None])
    k_dec = jnp.exp((chunk_len - 1.0 - idx)[None, :] * log_gamma[:, None])
    bcast = lambda t: jnp.broadcast_to(t[:, :, None], (RET_HEADS, chunk_len, RET_DK))
    g_len = jnp.exp(chunk_len * log_gamma)
    return g_len, decay, bcast(q_dec), bcast(k_dec)


def _retention(state0, rq, rk, rv, chunk_len):
    b, s, _ = rq.shape
    g_len, decay, q_dec, k_dec = _retention_tables(chunk_len)
    seq = pl.BlockSpec((b, chunk_len, RET_W), lambda c: (0, c, 0))
    whole4 = pl.BlockSpec((b, RET_HEADS, RET_DK, RET_DK), lambda c: (0, 0, 0, 0))
    tab = lambda n: pl.BlockSpec((RET_HEADS, chunk_len, n), lambda c: (0, 0, 0))
    return pl.pallas_call(
        _retention_kernel,
        grid=(s // chunk_len,),
        in_specs=[pl.BlockSpec(memory_space=pltpu.SMEM), whole4, seq, seq, seq,
                  tab(chunk_len), tab(RET_DK), tab(RET_DK)],
        out_specs=[seq, whole4],
        out_shape=[jax.ShapeDtypeStruct((b, s, RET_W), _F32),
                   jax.ShapeDtypeStruct((b, RET_HEADS, RET_DK, RET_DK), _F32)],
        compiler_params=pltpu.CompilerParams(dimension_semantics=("arbitrary",)),
        name="retention",
    )(g_len, state0, rq, rk, rv, decay, q_dec, k_dec)


def _top16_rows(x, n_rows):
    rows = lax.broadcasted_iota(jnp.int32, x.shape, 0)
    vals, idxs = [], []
    for _ in range(PEER_TOPK):
        m = jnp.max(x, axis=0, keepdims=True)
        idx = jnp.min(jnp.where(x == m, rows, n_rows), axis=0, keepdims=True)
        vals.append(m)
        idxs.append(idx)
        x = jnp.where(rows == idx, -jnp.inf, x)
    return vals, idxs


def _mix_route_kernel(x_ref, oatt_ref, oret_ref, rg_ref, lnin_g_ref, lnin_b_ref, gn_ref, wout_ref,
                      ln1_g_ref, ln1_b_ref, wq_ref, sk_ref,
                      h1_ref, eidx_ref, gate_ref, qs_ref):
    h = _layer_norm(x_ref[...], lnin_g_ref[...], lnin_b_ref[...])
    oret = oret_ref[...]
    rg = rg_ref[...]
    gn = gn_ref[...]
    ys = []
    for hh in range(RET_HEADS):
        sl = slice(hh * RET_DK, (hh + 1) * RET_DK)
        seg = oret[:, sl]
        mu = jnp.mean(seg, axis=-1, keepdims=True)
        sc = seg - mu
        var = jnp.mean(sc * sc, axis=-1, keepdims=True)
        gate = rg[:, sl]
        ys.append(sc * lax.rsqrt(var + LN_EPS) * gn[:, sl] * (gate * jax.nn.sigmoid(gate)))
    mixed = jnp.concatenate([oatt_ref[...]] + ys, axis=-1).astype(_BF16)
    mix = jnp.dot(mixed, wout_ref[...], preferred_element_type=_F32)
    h1 = _layer_norm(ALPHA * h + mix, ln1_g_ref[...], ln1_b_ref[...])
    h1_ref[...] = h1

    qp = jnp.dot(h1.astype(_BF16), wq_ref[...], preferred_element_type=_F32).astype(_BF16)
    for p in range(PEER_HEADS):
        qs_ref[p] = qp[:, p * LANES:(p + 1) * LANES]

    tm = x_ref.shape[0]
    row16 = lax.broadcasted_iota(jnp.int32, (PEER_TOPK, tm), 0)

    def head_body(p, carry):
        sc_t = lax.dot_general(sk_ref[p], qs_ref[p], (((1,), (1,)), ((), ())),
                               preferred_element_type=_F32)
        s1, i1 = _top16_rows(sc_t[:N_KEYS], N_KEYS)
        s2, i2 = _top16_rows(sc_t[N_KEYS:], N_KEYS)
        s2c = jnp.concatenate(s2, axis=0)
        i1c = jnp.concatenate(i1, axis=0)
        i2c = jnp.concatenate(i2, axis=0)
        cand = jnp.concatenate([s1[a] + s2c for a in range(PEER_TOPK)], axis=0)
        top, sel = _top16_rows(cand, PEER_TOPK * PEER_TOPK)
        e_rows = []
        for r in range(PEER_TOPK):
            a = sel[r] >> 4
            b = sel[r] & (PEER_TOPK - 1)
            e1 = jnp.sum(jnp.where(row16 == a, i1c, 0), axis=0, keepdims=True)
            e2 = jnp.sum(jnp.where(row16 == b, i2c, 0), axis=0, keepdims=True)
            e_rows.append(e1 * N_KEYS + e2)
        topc = jnp.concatenate(top, axis=0)
        ex = jnp.exp(topc - top[0])
        gate = ex / jnp.sum(ex, axis=0, keepdims=True)
        off = pl.multiple_of(p * PEER_TOPK, PEER_TOPK)
        eidx_ref[pl.ds(off, PEER_TOPK), :] = jnp.concatenate(e_rows, axis=0)
        gate_ref[pl.ds(off, PEER_TOPK), :] = gate
        return carry

    lax.fori_loop(0, PEER_HEADS, head_body, 0)


def _mix_route(x2d, o_att, o_ret, rg, lnin_g, lnin_b, gn_g, w_out_bf, ln1_g, ln1_b, wq_bf, sk_blk):
    t = x2d.shape[0]
    tm = ROUTE_TM
    row = lambda i: (i, 0)
    const = lambda i: (0, 0)
    vec = pl.BlockSpec((1, D_MODEL), const)
    return pl.pallas_call(
        _mix_route_kernel,
        grid=(t // tm,),
        in_specs=[
            pl.BlockSpec((tm, D_MODEL), row),
            pl.BlockSpec((tm, ATT_W), row),
            pl.BlockSpec((tm, RET_W), row),
            pl.BlockSpec((tm, RET_W), row),
            vec, vec,
            pl.BlockSpec((1, RET_W), const),
            pl.BlockSpec((D_MODEL, D_MODEL), const),
            vec, vec,
            pl.BlockSpec((D_MODEL, D_MODEL), const),
            pl.BlockSpec((PEER_HEADS, 2 * N_KEYS, LANES), lambda i: (0, 0, 0)),
        ],
        out_specs=[
            pl.BlockSpec((tm, D_MODEL), row),
            pl.BlockSpec((N_SEL, tm), lambda i: (0, i)),
            pl.BlockSpec((N_SEL, tm), lambda i: (0, i)),
        ],
        out_shape=[
            jax.ShapeDtypeStruct((t, D_MODEL), _F32),
            jax.ShapeDtypeStruct((N_SEL, t), jnp.int32),
            jax.ShapeDtypeStruct((N_SEL, t), _F32),
        ],
        scratch_shapes=[pltpu.VMEM((PEER_HEADS, tm, LANES), _BF16)],
        compiler_params=_vmem_limit(40 * 2 ** 20),
        name="mix_route",
    )(x2d, o_att, o_ret, rg, lnin_g, lnin_b, gn_g, w_out_bf, ln1_g, ln1_b, wq_bf, sk_blk)


def _peer_apply_kernel(idx_hbm, h1_ref, gate_ref, ln2_g_ref, ln2_b_ref, uv_hbm,
                       y_ref, uvbuf, idx_smem, gsem, isem, xs_ref, w_ref, o_buf, *, n_tiles):
    tb = PEER_TB
    lag = PEER_SLOTS - 1
    j = pl.program_id(0)

    def idx_copy(tile, buf):
        return pltpu.make_async_copy(idx_hbm.at[tile], idx_smem.at[buf], isem.at[buf])

    @pl.when(j == 0)
    def _():
        idx_copy(0, 0).start()

    @pl.when(j < n_tiles)
    def _():
        ibuf = j % 2
        idx_copy(j, ibuf).wait()

        @pl.when(j + 1 < n_tiles)
        def _():
            idx_copy(j + 1, 1 - ibuf).start()

        slot = j % PEER_SLOTS

        def issue_tok(tok, carry):
            for k in range(N_SEL):
                e = idx_smem[ibuf, tok, k]
                pltpu.make_async_copy(uv_hbm.at[e], uvbuf.at[slot, tok * N_SEL + k],
                                      gsem.at[slot]).start()
            return carry

        lax.fori_loop(0, tb, issue_tok, 0)

    @pl.when(j >= lag)
    def _():
        t = j - lag
        slot = t % PEER_SLOTS
        pltpu.make_async_copy(uv_hbm.at[pl.ds(0, tb * N_SEL)], uvbuf.at[slot], gsem.at[slot]).wait()

        h1 = h1_ref[...]
        for s in range(SUBLANES):
            xs_ref[s * tb:(s + 1) * tb, :] = h1[:, s * LANES:(s + 1) * LANES]
        gate_blk = gate_ref[...]
        lane = lax.broadcasted_iota(jnp.int32, gate_blk.shape, 1)
        lane0 = (t * tb) % LANES

        def tok_body(tok, carry):
            base = tok * N_SEL
            xt = xs_ref[pl.ds(tok, SUBLANES, stride=tb), :]
            acts = []
            for g in range(N_SEL // SUBLANES):
                u = uvbuf[slot, pl.ds(base + g * SUBLANES, SUBLANES), 0]
                part = jnp.sum(u * xt[None], axis=1)
                acts.append(jnp.sum(part, axis=1, keepdims=True))
            act = jnp.concatenate(acts, axis=0)
            gcol = jnp.sum(jnp.where(lane == lane0 + tok, gate_blk, 0.0), axis=1, keepdims=True)
            w = gcol * jax.nn.gelu(act)
            w_ref[...] = jnp.broadcast_to(w, (N_SEL, LANES))
            accs = [jnp.zeros(ROW_TILE, _F32) for _ in range(4)]
            for k in range(N_SEL):
                wk = jnp.broadcast_to(w_ref[k:k + 1, :], ROW_TILE)
                accs[k % 4] = accs[k % 4] + wk * uvbuf[slot, base + k, 1]
            o_buf[pl.ds(pl.multiple_of(tok * SUBLANES, SUBLANES), SUBLANES), :] = (
                (accs[0] + accs[1]) + (accs[2] + accs[3]))
            return carry

        lax.fori_loop(0, tb, tok_body, 0)

        out = jnp.concatenate([o_buf[pl.ds(s, tb, stride=SUBLANES), :] for s in range(SUBLANES)],
                              axis=-1)
        y_ref[...] = _layer_norm(ALPHA * h1 + out, ln2_g_ref[...], ln2_b_ref[...])


def _peer_apply(h1, eidx_t, gate_t, uv_tab, ln2_g, ln2_b):
    t = h1.shape[0]
    tb = PEER_TB
    n_tiles = t // tb
    lag = PEER_SLOTS - 1
    idx_tiles = eidx_t.T.reshape(n_tiles, tb, N_SEL)
    done = lambda j: jnp.maximum(j - lag, 0)
    const = lambda j: (0, 0)
    kern = functools.partial(_peer_apply_kernel, n_tiles=n_tiles)
    return pl.pallas_call(
        kern,
        grid=(n_tiles + lag,),
        in_specs=[
            pl.BlockSpec(memory_space=pl.ANY),
            pl.BlockSpec((tb, D_MODEL), lambda j: (done(j), 0)),
            pl.BlockSpec((N_SEL, LANES), lambda j: (0, done(j) * tb // LANES)),
            pl.BlockSpec((1, D_MODEL), const),
            pl.BlockSpec((1, D_MODEL), const),
            pl.BlockSpec(memory_space=pl.ANY),
        ],
        out_specs=pl.BlockSpec((tb, D_MODEL), lambda j: (done(j), 0)),
        out_shape=jax.ShapeDtypeStruct((t, D_MODEL), _F32),
        scratch_shapes=[
            pltpu.VMEM((PEER_SLOTS, tb * N_SEL, 2) + ROW_TILE, _F32),
            pltpu.SMEM((2, tb, N_SEL), jnp.int32),
            pltpu.SemaphoreType.DMA((PEER_SLOTS,)),
            pltpu.SemaphoreType.DMA((2,)),
            pltpu.VMEM((SUBLANES * tb, LANES), _F32),
            pltpu.VMEM((N_SEL, LANES), _F32),
            pltpu.VMEM((tb * SUBLANES, LANES), _F32),
        ],
        compiler_params=pltpu.CompilerParams(
            dimension_semantics=("arbitrary",),
            vmem_limit_bytes=int(PEER_SLOTS * tb * N_SEL * 2 * 4096 + 8 * 2 ** 20)),
        name="peer_apply",
    )(idx_tiles, h1, gate_t, ln2_g, ln2_b, uv_tab)


def _rope_tables(pos):
    half = RET_DK // 2
    inv = ROPE_BASE ** (-jnp.arange(half, dtype=_F32) / half)
    ang = jnp.asarray(pos, _F32)[:, None] * inv[None, :]
    cos = jnp.cos(ang)
    sin = jnp.sin(ang)
    return jnp.concatenate([cos, cos], axis=-1), jnp.concatenate([-sin, sin], axis=-1)


def _subkey_blocks(sub_keys):
    z = jnp.zeros((PEER_HEADS, N_KEYS, PEER_DK_HALF), sub_keys.dtype)
    top = jnp.concatenate([sub_keys[:, 0], z], axis=-1)
    bot = jnp.concatenate([z, sub_keys[:, 1]], axis=-1)
    return jnp.concatenate([top, bot], axis=1).astype(_BF16)


def kernel(x_prompt, x_sample, cache_meta_k, cache_meta_v, cache_swa_k, cache_swa_v, state_ret,
           meta_tokens, ln_in_g, ln_in_b, rel_bias, w_in, w_out, attn_sinks, ret_gn_g, ln1_g, ln1_b,
           peer_wq, peer_subkeys, peer_u, peer_v, ln2_g, ln2_b):
    b, s, _ = x_prompt.shape
    db, ds, _ = x_sample.shape
    row = lambda a: a.reshape(1, -1)

    w_in_bf = w_in[0].astype(_BF16)
    w_out_bf = w_out[0].astype(_BF16)
    wq_bf = peer_wq[0].astype(_BF16)
    sk_blk = _subkey_blocks(peer_subkeys[0])
    uv_tab = jnp.stack([peer_u[0].reshape((N_EXPERTS,) + ROW_TILE),
                        peer_v[0].reshape((N_EXPERTS,) + ROW_TILE)], axis=1)
    lnin_g, lnin_b = row(ln_in_g), row(ln_in_b)
    cos_m, sin_m = _rope_tables(np.arange(N_META))
    cos_p, sin_p = _rope_tables(N_META + np.arange(s))
    cos_s, sin_s = _rope_tables(N_META + PAST_LEN + np.arange(ds))
    bias_tab = _rel_bias_tables(rel_bias)
    sinks = attn_sinks[0]

    _, kv_m, rq_m, rk_m, rv_m, _ = _ln_proj(meta_tokens.astype(x_prompt.dtype), N_META, lnin_g, lnin_b,
                                         w_in_bf, cos_m, sin_m, N_META)
    xp2 = x_prompt.reshape(b * s, D_MODEL)
    xs2 = x_sample.reshape(db * ds, D_MODEL)
    q_p, kv_p, rq_p, rk_p, rv_p, rg_p = _ln_proj(xp2, s, lnin_g, lnin_b, w_in_bf, cos_p, sin_p, PROJ_TM)
    q_s, kv_s, rq_s, rk_s, rv_s, rg_s = _ln_proj(xs2, ds, lnin_g, lnin_b, w_in_bf, cos_s, sin_s, ds)

    kv_p3 = kv_p.reshape(b, s, 2 * KV_W)
    o_att_p = _attention_prompt(q_p.reshape(b, s, ATT_W), kv_p3, kv_m, bias_tab, sinks)
    kv_s3 = kv_s.reshape(db, ds, 2 * KV_W)
    o_att_s = _attention_sample(
        q_s.reshape(db, ds, ATT_W), kv_s3,
        cache_swa_k[0].reshape(db, WINDOW, KV_W), cache_swa_v[0].reshape(db, WINDOW, KV_W),
        cache_meta_k[0].reshape(db, N_META, KV_W), cache_meta_v[0].reshape(db, N_META, KV_W),
        bias_tab, sinks)

    zero_state = jnp.zeros((1, RET_HEADS, RET_DK, RET_DK), _F32)
    _, st_meta = _retention(zero_state, rq_m[None], rk_m[None], rv_m[None], N_META)
    st0_p = jnp.broadcast_to(st_meta, (b, RET_HEADS, RET_DK, RET_DK))
    seq3 = lambda a, n, l: a.reshape(n, l, RET_W)
    o_ret_p, st_p = _retention(st0_p, seq3(rq_p, b, s), seq3(rk_p, b, s), seq3(rv_p, b, s), CHUNK)
    o_ret_s, st_s = _retention(state_ret[0].astype(_F32), seq3(rq_s, db, ds), seq3(rk_s, db, ds),
                               seq3(rv_s, db, ds), ds)

    def tail(x2d, o_att, o_ret, rg):
        h1, eidx_t, gate_t = _mix_route(x2d, o_att, o_ret, rg, lnin_g, lnin_b, row(ret_gn_g[0]),
                                        w_out_bf, row(ln1_g[0]), row(ln1_b[0]), wq_bf, sk_blk)
        return _peer_apply(h1, eidx_t, gate_t, uv_tab, row(ln2_g[0]), row(ln2_b[0]))

    y_p = tail(xp2, o_att_p.reshape(b * s, ATT_W), o_ret_p.reshape(b * s, RET_W), rg_p)
    y_s = tail(xs2, o_att_s.reshape(db * ds, ATT_W), o_ret_s.reshape(db * ds, RET_W), rg_s)

    kvh = lambda a, n, l: a.reshape(1, n, l, KV_HEADS, HEAD_DIM)
    meta_k = jnp.broadcast_to(kvh(kv_m[:, :KV_W], 1, N_META), (1, b, N_META, KV_HEADS, HEAD_DIM))
    meta_v = jnp.broadcast_to(kvh(kv_m[:, KV_W:], 1, N_META), (1, b, N_META, KV_HEADS, HEAD_DIM))
    tail_kv = kv_p3[:, s - WINDOW:]
    return (y_p.reshape(b, s, D_MODEL), y_s.reshape(db, ds, D_MODEL),
            meta_k, meta_v,
            kvh(tail_kv[..., :KV_W], b, WINDOW), kvh(tail_kv[..., KV_W:], b, WINDOW),
            st_p[None],
            kvh(kv_s3[..., :KV_W], db, ds), kvh(kv_s3[..., KV_W:], db, ds),
            st_s[None])
```

```python
import functools
import math

import jax
import jax.numpy as jnp
import numpy as np
from jax import lax
from jax.experimental import pallas as pl
from jax.experimental.pallas import tpu as pltpu

D_MODEL = 1024
CHUNK = 64
N_META = 16
PAST_LEN = 2048
ATT_W = 512
HEAD_DIM = 64
N_HEADS = 8
KV_HEADS = 2
KV_W = KV_HEADS * HEAD_DIM
WINDOW = 128
N_BACK = WINDOW // CHUNK
BAND = (N_BACK + 1) * CHUNK
N_KEYS_ATT = BAND + N_META
N_BUCKETS = 32
MAX_DISTANCE = 128
RET_W = 512
RET_HEADS = 4
RET_DK = 128
ROPE_BASE = 10000.0
PROJ_W = 2816
N_KEYS = 128
N_EXPERTS = N_KEYS * N_KEYS
PEER_HEADS = 8
PEER_TOPK = 16
PEER_DK_HALF = 64
N_SEL = PEER_HEADS * PEER_TOPK
ALPHA = 2.0 ** 0.25
LN_EPS = 1e-5
NEG_INF = -1e30

LANES = 128
SUBLANES = 8
ROW_TILE = (SUBLANES, LANES)

PEER_TB = 8
PEER_SLOTS = 3
ROUTE_TM = 256
PROJ_TM = 256

_F32 = jnp.float32
_BF16 = jnp.bfloat16


def _vmem_limit(nbytes):
    return pltpu.CompilerParams(vmem_limit_bytes=int(nbytes))


def _layer_norm(x, g, b):
    mu = jnp.mean(x, axis=-1, keepdims=True)
    xc = x - mu
    var = jnp.mean(xc * xc, axis=-1, keepdims=True)
    return xc * lax.rsqrt(var + LN_EPS) * g + b


def _ln_proj_kernel(x_ref, g_ref, b_ref, w_ref, cos_ref, sin_ref,
                    q_ref, kv_ref, rq_ref, rk_ref, rv_ref, rg_ref):
    h = _layer_norm(x_ref[...], g_ref[...], b_ref[...])
    p = jnp.dot(h.astype(_BF16), w_ref[...], preferred_element_type=_F32)
    cos = cos_ref[...]
    sin = sin_ref[...]

    def rotary(a):
        outs = []
        for hh in range(RET_HEADS):
            seg = a[:, hh * RET_DK:(hh + 1) * RET_DK]
            outs.append(seg * cos + pltpu.roll(seg, RET_DK // 2, axis=1) * sin)
        return jnp.concatenate(outs, axis=-1)

    q_ref[...] = p[:, 0:512]
    kv_ref[...] = p[:, 512:768]
    rq_ref[...] = rotary(p[:, 768:1280])
    rk_ref[...] = rotary(p[:, 1280:1792]) * (RET_DK ** -0.5)
    rv_ref[...] = p[:, 1792:2304]
    rg_ref[...] = p[:, 2304:2816]


def _ln_proj(x2d, seq_len, ln_g, ln_b, w_in_bf, cos, sin, tm):
    t = x2d.shape[0]
    nblk_s = seq_len // tm
    row = lambda i: (i, 0)
    const = lambda i: (0, 0)
    outs = [jax.ShapeDtypeStruct((t, w), _F32) for w in (512, 256, 512, 512, 512, 512)]
    return pl.pallas_call(
        _ln_proj_kernel,
        grid=(t // tm,),
        in_specs=[
            pl.BlockSpec((tm, D_MODEL), row),
            pl.BlockSpec((1, D_MODEL), const),
            pl.BlockSpec((1, D_MODEL), const),
            pl.BlockSpec((D_MODEL, PROJ_W), const),
            pl.BlockSpec((tm, RET_DK), lambda i: (i % nblk_s, 0)),
            pl.BlockSpec((tm, RET_DK), lambda i: (i % nblk_s, 0)),
        ],
        out_specs=[pl.BlockSpec((tm, w), row) for w in (512, 256, 512, 512, 512, 512)],
        out_shape=outs,
        compiler_params=_vmem_limit(48 * 2 ** 20),
        name="ln_proj",
    )(x2d, ln_g, ln_b, w_in_bf, cos, sin)


def _t5_bucket_np(rel):
    nb = N_BUCKETS // 2
    max_exact = nb // 2
    n = np.abs(rel)
    large = max_exact + (np.log(np.maximum(n, max_exact).astype(np.float32) / max_exact)
                         / math.log(MAX_DISTANCE / max_exact) * (nb - max_exact)).astype(np.int32)
    large = np.minimum(large, nb - 1)
    return (np.where(rel > 0, nb, 0) + np.where(n < max_exact, n, large)).astype(np.int32)


def _bias_buckets():
    i = np.arange(CHUNK)
    jb = np.arange(BAND)
    m = np.arange(N_META)
    rel_band = jb[None, :] - N_BACK * CHUNK - i[:, None]
    out = []
    for c in range(3):
        rel_meta = m[None, :] - (N_META + c * CHUNK + i[:, None])
        out.append(np.concatenate([rel_band, rel_meta], axis=-1))
    return _t5_bucket_np(np.stack(out))


def _rel_bias_kernel(rb_ref, bucket_ref, o_ref):
    h = pl.program_id(1)
    bucket = bucket_ref[0]
    acc = jnp.zeros(bucket.shape, _F32)
    for v in range(N_BUCKETS):
        acc = jnp.where(bucket == v, rb_ref[v, h], acc)
    o_ref[0, 0] = acc


def _rel_bias_tables(rel_bias):
    buckets = jnp.asarray(_bias_buckets())
    return pl.pallas_call(
        _rel_bias_kernel,
        grid=(3, N_HEADS),
        in_specs=[
            pl.BlockSpec(memory_space=pltpu.SMEM),
            pl.BlockSpec((1, CHUNK, N_KEYS_ATT), lambda v, h: (v, 0, 0)),
        ],
        out_specs=pl.BlockSpec((1, 1, CHUNK, N_KEYS_ATT), lambda v, h: (v, h, 0, 0)),
        out_shape=jax.ShapeDtypeStruct((3, N_HEADS, CHUNK, N_KEYS_ATT), _F32),
        name="rel_bias",
    )(rel_bias, buckets)


def _attend(q, kk_band, vv_band, kk_meta, vv_meta, bias_ref, sinks_ref, min_valid_col):
    lane = lax.broadcasted_iota(jnp.int32, (1, LANES), 1)
    lo = lane < HEAD_DIM
    kk = jnp.concatenate([kk_band, kk_meta], axis=0)
    vv = jnp.concatenate([vv_band, vv_meta], axis=0)
    kk_r = pltpu.roll(kk, HEAD_DIM, axis=1)
    vv_r = pltpu.roll(vv, HEAD_DIM, axis=1)
    k_dup = [jnp.where(lo, kk, kk_r).astype(_BF16), jnp.where(lo, kk_r, kk).astype(_BF16)]
    v_lo = [jnp.where(lo, vv, 0.0).astype(_BF16), jnp.where(lo, vv_r, 0.0).astype(_BF16)]
    v_hi = [jnp.where(lo, 0.0, vv_r).astype(_BF16), jnp.where(lo, 0.0, vv).astype(_BF16)]
    col = lax.broadcasted_iota(jnp.int32, (CHUNK, N_KEYS_ATT), 1)
    valid = col >= min_valid_col
    outs = []
    for pair in range(N_HEADS // 2):
        kvh = pair // 2
        q2 = q[:, pair * LANES:(pair + 1) * LANES]
        o_pair = jnp.zeros((CHUNK, LANES), _F32)
        for half in range(2):
            head = 2 * pair + half
            qm = (jnp.where(lo, q2, 0.0) if half == 0 else jnp.where(lo, 0.0, q2)).astype(_BF16)
            s = lax.dot_general(qm, k_dup[kvh], (((1,), (1,)), ((), ())),
                                preferred_element_type=_F32)
            s = s * (HEAD_DIM ** -0.5) + bias_ref[head]
            s = jnp.where(valid, s, NEG_INF)
            sink = sinks_ref[head]
            m = jnp.maximum(jnp.max(s, axis=-1, keepdims=True), sink)
            p = jnp.exp(s - m)
            denom = jnp.sum(p, axis=-1, keepdims=True) + jnp.exp(sink - m)
            pn = (p / denom).astype(_BF16)
            vsel = v_lo[kvh] if half == 0 else v_hi[kvh]
            o_pair = o_pair + jnp.dot(pn, vsel, preferred_element_type=_F32)
        outs.append(o_pair)
    return jnp.concatenate(outs, axis=-1)


def _attn_prompt_kernel(sinks_ref, q_ref, kv2_ref, kv1_ref, kv0_ref, meta_ref, bias_ref, o_ref):
    c = pl.program_id(1)
    kvs = [kv2_ref[0], kv1_ref[0], kv0_ref[0]]
    kk_band = jnp.concatenate([kv[:, :KV_W] for kv in kvs], axis=0)
    vv_band = jnp.concatenate([kv[:, KV_W:] for kv in kvs], axis=0)
    meta = meta_ref[...]
    min_valid = (N_BACK - jnp.minimum(c, N_BACK)) * CHUNK
    o_ref[0] = _attend(q_ref[0], kk_band, vv_band, meta[:, :KV_W], meta[:, KV_W:],
                       bias_ref.at[0], sinks_ref, min_valid)


def _attention_prompt(q, kv, kv_meta, bias_tab, sinks):
    b, s, _ = q.shape
    nc = s // CHUNK

    def back(j):
        return lambda bi, c: (bi, jnp.maximum(c - j, 0), 0)

    return pl.pallas_call(
        _attn_prompt_kernel,
        grid=(b, nc),
        in_specs=[
            pl.BlockSpec(memory_space=pltpu.SMEM),
            pl.BlockSpec((1, CHUNK, ATT_W), lambda bi, c: (bi, c, 0)),
            pl.BlockSpec((1, CHUNK, 2 * KV_W), back(2)),
            pl.BlockSpec((1, CHUNK, 2 * KV_W), back(1)),
            pl.BlockSpec((1, CHUNK, 2 * KV_W), back(0)),
            pl.BlockSpec((N_META, 2 * KV_W), lambda bi, c: (0, 0)),
            pl.BlockSpec((1, N_HEADS, CHUNK, N_KEYS_ATT),
                         lambda bi, c: (jnp.minimum(c, N_BACK), 0, 0, 0)),
        ],
        out_specs=pl.BlockSpec((1, CHUNK, ATT_W), lambda bi, c: (bi, c, 0)),
        out_shape=jax.ShapeDtypeStruct((b, s, ATT_W), _F32),
        name="attn_prompt",
    )(sinks, q, kv, kv, kv, kv_meta, bias_tab)


def _attn_sample_kernel(sinks_ref, q_ref, kv_ref, ck_ref, cv_ref, mk_ref, mv_ref, bias_ref, o_ref):
    kv = kv_ref[0]
    kk_band = jnp.concatenate([ck_ref[0], kv[:, :KV_W]], axis=0)
    vv_band = jnp.concatenate([cv_ref[0], kv[:, KV_W:]], axis=0)
    o_ref[0] = _attend(q_ref[0], kk_band, vv_band, mk_ref[0], mv_ref[0],
                       bias_ref.at[0], sinks_ref, 0)


def _attention_sample(q, kv, cache_k, cache_v, meta_k, meta_v, bias_tab, sinks):
    b = q.shape[0]
    per_b = lambda bi: (bi, 0, 0)
    return pl.pallas_call(
        _attn_sample_kernel,
        grid=(b,),
        in_specs=[
            pl.BlockSpec(memory_space=pltpu.SMEM),
            pl.BlockSpec((1, CHUNK, ATT_W), per_b),
            pl.BlockSpec((1, CHUNK, 2 * KV_W), per_b),
            pl.BlockSpec((1, WINDOW, KV_W), per_b),
            pl.BlockSpec((1, WINDOW, KV_W), per_b),
            pl.BlockSpec((1, N_META, KV_W), per_b),
            pl.BlockSpec((1, N_META, KV_W), per_b),
            pl.BlockSpec((1, N_HEADS, CHUNK, N_KEYS_ATT), lambda bi: (N_BACK, 0, 0, 0)),
        ],
        out_specs=pl.BlockSpec((1, CHUNK, ATT_W), per_b),
        out_shape=jax.ShapeDtypeStruct((b, CHUNK, ATT_W), _F32),
        name="attn_sample",
    )(sinks, q, kv, cache_k, cache_v, meta_k, meta_v, bias_tab)


def _retention_kernel(gl_ref, st0_ref, rq_ref, rk_ref, rv_ref, decay_ref, qdec_ref, kdec_ref,
                      o_ref, st_ref):
    @pl.when(pl.program_id(0) == 0)
    def _():
        st_ref[...] = st0_ref[...]

    nb = rq_ref.shape[0]
    nt_dims = (((1,), (1,)), ((), ()))
    tn_dims = (((0,), (0,)), ((), ()))
    for b in range(nb):
        outs = []
        for h in range(RET_HEADS):
            sl = slice(h * RET_DK, (h + 1) * RET_DK)
            q = rq_ref[b, :, sl]
            k = rk_ref[b, :, sl]
            v = rv_ref[b, :, sl]
            qb = q.astype(_BF16)
            vb = v.astype(_BF16)
            s = lax.dot_general(qb, k.astype(_BF16), nt_dims, preferred_element_type=_F32)
            s = s * decay_ref[h]
            inner = jnp.dot(s.astype(_BF16), vb, preferred_element_type=_F32)
            st = st_ref[b, h]
            cross = jnp.dot(qb, st.astype(_BF16), preferred_element_type=_F32) * qdec_ref[h]
            outs.append(inner + cross)
            kd = (k * kdec_ref[h]).astype(_BF16)
            upd = lax.dot_general(kd, vb, tn_dims, preferred_element_type=_F32)
            st_ref[b, h] = gl_ref[h] * st + upd
        o_ref[b] = jnp.concatenate(outs, axis=-1)


def _retention_tables(chunk_len):
    log_gamma = jnp.log(1.0 - 2.0 ** (-5.0 - jnp.arange(RET_HEADS, dtype=_F32)))
    idx = jnp.arange(chunk_len, dtype=_F32)
    diff = idx[:, None] - idx[None, :]
    decay = jnp.where(diff >= 0, jnp.exp(jnp.maximum(diff, 0.0)[None] * log_gamma[:, None, None]), 0.0)
    q_dec = jnp.exp((idx + 1.0)[None, :] * log_gamma[:, None])
    k_dec = jnp.exp((chunk_len - 1.0 - idx)[None, :] * log_gamma[:, None])
    bcast = lambda t: jnp.broadcast_to(t[:, :, None], (RET_HEADS, chunk_len, RET_DK))
    g_len = jnp.exp(chunk_len * log_gamma)
    return g_len, decay, bcast(q_dec), bcast(k_dec)


def _retention(state0, rq, rk, rv, chunk_len):
    b, s, _ = rq.shape
    g_len, decay, q_dec, k_dec = _retention_tables(chunk_len)
    seq = pl.BlockSpec((b, chunk_len, RET_W), lambda c: (0, c, 0))
    whole4 = pl.BlockSpec((b, RET_HEADS, RET_DK, RET_DK), lambda c: (0, 0, 0, 0))
    tab = lambda n: pl.BlockSpec((RET_HEADS, chunk_len, n), lambda c: (0, 0, 0))
    return pl.pallas_call(
        _retention_kernel,
        grid=(s // chunk_len,),
        in_specs=[pl.BlockSpec(memory_space=pltpu.SMEM), whole4, seq, seq, seq,
                  tab(chunk_len), tab(RET_DK), tab(RET_DK)],
        out_specs=[seq, whole4],
        out_shape=[jax.ShapeDtypeStruct((b, s, RET_W), _F32),
                   jax.ShapeDtypeStruct((b, RET_HEADS, RET_DK, RET_DK), _F32)],
        compiler_params=pltpu.CompilerParams(dimension_semantics=("arbitrary",)),
        name="retention",
    )(g_len, state0, rq, rk, rv, decay, q_dec, k_dec)


def _top16_rows(x, n_rows):
    rows = lax.broadcasted_iota(jnp.int32, x.shape, 0)
    vals, idxs = [], []
    for _ in range(PEER_TOPK):
        m = jnp.max(x, axis=0, keepdims=True)
        idx = jnp.min(jnp.where(x == m, rows, n_rows), axis=0, keepdims=True)
        vals.append(m)
        idxs.append(idx)
        x = jnp.where(rows == idx, -jnp.inf, x)
    return vals, idxs


def _mix_route_kernel(x_ref, oatt_ref, oret_ref, rg_ref, lnin_g_ref, lnin_b_ref, gn_ref, wout_ref,
                      ln1_g_ref, ln1_b_ref, wq_ref, sk_ref,
                      h1_ref, eidx_ref, gate_ref, qs_ref):
    h = _layer_norm(x_ref[...], lnin_g_ref[...], lnin_b_ref[...])
    oret = oret_ref[...]
    rg = rg_ref[...]
    gn = gn_ref[...]
    ys = []
    for hh in range(RET_HEADS):
        sl = slice(hh * RET_DK, (hh + 1) * RET_DK)
        seg = oret[:, sl]
        mu = jnp.mean(seg, axis=-1, keepdims=True)
        sc = seg - mu
        var = jnp.mean(sc * sc, axis=-1, keepdims=True)
        gate = rg[:, sl]
        ys.append(sc * lax.rsqrt(var + LN_EPS) * gn[:, sl] * (gate * jax.nn.sigmoid(gate)))
    mixed = jnp.concatenate([oatt_ref[...]] + ys, axis=-1).astype(_BF16)
    mix = jnp.dot(mixed, wout_ref[...], preferred_element_type=_F32)
    h1 = _layer_norm(ALPHA * h + mix, ln1_g_ref[...], ln1_b_ref[...])
    h1_ref[...] = h1

    qp = jnp.dot(h1.astype(_BF16), wq_ref[...], preferred_element_type=_F32).astype(_BF16)
    for p in range(PEER_HEADS):
        qs_ref[p] = qp[:, p * LANES:(p + 1) * LANES]

    tm = x_ref.shape[0]
    row16 = lax.broadcasted_iota(jnp.int32, (PEER_TOPK, tm), 0)

    def head_body(p, carry):
        sc_t = lax.dot_general(sk_ref[p], qs_ref[p], (((1,), (1,)), ((), ())),
                               preferred_element_type=_F32)
        s1, i1 = _top16_rows(sc_t[:N_KEYS], N_KEYS)
        s2, i2 = _top16_rows(sc_t[N_KEYS:], N_KEYS)
        s2c = jnp.concatenate(s2, axis=0)
        i1c = jnp.concatenate(i1, axis=0)
        i2c = jnp.concatenate(i2, axis=0)
        cand = jnp.concatenate([s1[a] + s2c for a in range(PEER_TOPK)], axis=0)
        top, sel = _top16_rows(cand, PEER_TOPK * PEER_TOPK)
        e_rows = []
        for r in range(PEER_TOPK):
            a = sel[r] >> 4
            b = sel[r] & (PEER_TOPK - 1)
            e1 = jnp.sum(jnp.where(row16 == a, i1c, 0), axis=0, keepdims=True)
            e2 = jnp.sum(jnp.where(row16 == b, i2c, 0), axis=0, keepdims=True)
            e_rows.append(e1 * N_KEYS + e2)
        topc = jnp.concatenate(top, axis=0)
        ex = jnp.exp(topc - top[0])
        gate = ex / jnp.sum(ex, axis=0, keepdims=True)
        off = pl.multiple_of(p * PEER_TOPK, PEER_TOPK)
        eidx_ref[pl.ds(off, PEER_TOPK), :] = jnp.concatenate(e_rows, axis=0)
        gate_ref[pl.ds(off, PEER_TOPK), :] = gate
        return carry

    lax.fori_loop(0, PEER_HEADS, head_body, 0)


def _mix_route(x2d, o_att, o_ret, rg, lnin_g, lnin_b, gn_g, w_out_bf, ln1_g, ln1_b, wq_bf, sk_blk):
    t = x2d.shape[0]
    tm = ROUTE_TM
    row = lambda i: (i, 0)
    const = lambda i: (0, 0)
    vec = pl.BlockSpec((1, D_MODEL), const)
    return pl.pallas_call(
        _mix_route_kernel,
        grid=(t // tm,),
        in_specs=[
            pl.BlockSpec((tm, D_MODEL), row),
            pl.BlockSpec((tm, ATT_W), row),
            pl.BlockSpec((tm, RET_W), row),
            pl.BlockSpec((tm, RET_W), row),
            vec, vec,
            pl.BlockSpec((1, RET_W), const),
            pl.BlockSpec((D_MODEL, D_MODEL), const),
            vec, vec,
            pl.BlockSpec((D_MODEL, D_MODEL), const),
            pl.BlockSpec((PEER_HEADS, 2 * N_KEYS, LANES), lambda i: (0, 0, 0)),
        ],
        out_specs=[
            pl.BlockSpec((tm, D_MODEL), row),
            pl.BlockSpec((N_SEL, tm), lambda i: (0, i)),
            pl.BlockSpec((N_SEL, tm), lambda i: (0, i)),
        ],
        out_shape=[
            jax.ShapeDtypeStruct((t, D_MODEL), _F32),
            jax.ShapeDtypeStruct((N_SEL, t), jnp.int32),
            jax.ShapeDtypeStruct((N_SEL, t), _F32),
        ],
        scratch_shapes=[pltpu.VMEM((PEER_HEADS, tm, LANES), _BF16)],
        compiler_params=_vmem_limit(40 * 2 ** 20),
        name="mix_route",
    )(x2d, o_att, o_ret, rg, lnin_g, lnin_b, gn_g, w_out_bf, ln1_g, ln1_b, wq_bf, sk_blk)


_BITREV8 = (0, 4, 2, 6, 1, 5, 3, 7)


def _rows_to_sublanes(tiles, sub):
    m4 = sub < 4
    m2 = (sub & 2) == 0
    m1 = (sub & 1) == 0
    t = [tiles[i] for i in _BITREV8]
    c = []
    for a, b in ((0, 1), (2, 3), (4, 5), (6, 7)):
        w = jnp.where(m4, t[a], t[b])
        x = jnp.where(m4, t[b], t[a])
        c.append(w + pltpu.roll(x, 4, axis=0))
    d = []
    for a, b in ((0, 1), (2, 3)):
        d.append(jnp.where(m2, c[a] + pltpu.roll(c[a], 6, axis=0), c[b] + pltpu.roll(c[b], 2, axis=0)))
    return jnp.where(m1, d[0] + pltpu.roll(d[0], 7, axis=0), d[1] + pltpu.roll(d[1], 1, axis=0))


def _peer_apply_kernel(idx_hbm, h1_ref, gate_ref, ln2_g_ref, ln2_b_ref, uv_hbm,
                       y_ref, uvbuf, idx_smem, gsem, isem, xs_ref, wb_ref, o_buf, *, n_tiles):
    tb = PEER_TB
    lag = PEER_SLOTS - 1
    j = pl.program_id(0)

    def idx_copy(tile, buf):
        return pltpu.make_async_copy(idx_hbm.at[tile], idx_smem.at[buf], isem.at[buf])

    @pl.when(j == 0)
    def _():
        idx_copy(0, 0).start()

    issue_on = j < n_tiles
    comp_on = j >= lag
    ibuf = j % 2
    islot = j % PEER_SLOTS
    t = j - lag
    cslot = t % PEER_SLOTS
    lane0 = (t * tb) % LANES

    @pl.when(issue_on)
    def _():
        idx_copy(j, ibuf).wait()

        @pl.when(j + 1 < n_tiles)
        def _():
            idx_copy(j + 1, 1 - ibuf).start()

    @pl.when(comp_on)
    def _():
        pltpu.make_async_copy(uv_hbm.at[pl.ds(0, tb * N_SEL)], uvbuf.at[cslot], gsem.at[cslot]).wait()
        h1 = h1_ref[...]
        for s in range(SUBLANES):
            xs_ref[s * tb:(s + 1) * tb, :] = h1[:, s * LANES:(s + 1) * LANES]

    def token_pass(do_issue, do_comp):
        n_groups = N_SEL // SUBLANES

        def make_issue(tok):
            pending = list(range(N_SEL)) if do_issue else []

            def issue(n):
                for _ in range(min(n, len(pending))):
                    k = pending.pop(0)
                    e = idx_smem[ibuf, tok, k]
                    pltpu.make_async_copy(uv_hbm.at[e], uvbuf.at[islot, tok * N_SEL + k],
                                          gsem.at[islot]).start()
            return issue

        if not do_comp:
            def issue_body(tok, carry):
                make_issue(tok)(N_SEL)
                return carry
            lax.fori_loop(0, tb, issue_body, 0)
            return

        issues = [make_issue(tok) for tok in range(tb)]
        sub = lax.broadcasted_iota(jnp.int32, ROW_TILE, 0)

        lane = lax.broadcasted_iota(jnp.int32, (N_SEL, LANES), 1)
        act = jnp.zeros((N_SEL, LANES), _F32)
        for tok in range(tb):
            xt = xs_ref[pl.ds(tok, SUBLANES, stride=tb), :]
            parts = []
            for g in range(n_groups):
                rows = [uvbuf[cslot, tok * N_SEL + g * SUBLANES + r, 0] for r in range(SUBLANES)]
                issues[tok](5)
                prods = [row * xt for row in rows]
                parts.append(jnp.sum(_rows_to_sublanes(prods, sub), axis=1, keepdims=True))
            act = jnp.where(lane == tok, jnp.concatenate(parts, axis=0), act)

        gate0 = pltpu.roll(gate_ref[...], LANES - lane0, axis=1)
        w = jax.nn.gelu(act) * gate0
        for tok in range(tb):
            wb_ref[:, tok * LANES:(tok + 1) * LANES] = jnp.broadcast_to(w[:, tok:tok + 1], (N_SEL, LANES))

        for tok in range(tb):
            accs = [jnp.zeros(ROW_TILE, _F32) for _ in range(4)]
            for k0 in range(0, N_SEL, SUBLANES):
                ks = range(k0, k0 + SUBLANES)
                wks = [jnp.broadcast_to(wb_ref[k:k + 1, tok * LANES:(tok + 1) * LANES], ROW_TILE) for k in ks]
                vks = [uvbuf[cslot, tok * N_SEL + k, 1] for k in ks]
                issues[tok](3)
                for k, wk, vk in zip(ks, wks, vks):
                    accs[k % 4] = accs[k % 4] + wk * vk
            issues[tok](N_SEL)
            o_buf[tok * SUBLANES:(tok + 1) * SUBLANES, :] = (accs[0] + accs[1]) + (accs[2] + accs[3])

    pl.when(jnp.logical_and(issue_on, jnp.logical_not(comp_on)))(lambda: token_pass(True, False))
    pl.when(jnp.logical_and(issue_on, comp_on))(lambda: token_pass(True, True))
    pl.when(jnp.logical_and(jnp.logical_not(issue_on), comp_on))(lambda: token_pass(False, True))

    @pl.when(comp_on)
    def _():
        out = jnp.concatenate([o_buf[pl.ds(s, tb, stride=SUBLANES), :] for s in range(SUBLANES)],
                              axis=-1)
        y_ref[...] = _layer_norm(ALPHA * h1_ref[...] + out, ln2_g_ref[...], ln2_b_ref[...])


def _peer_apply(h1, eidx_t, gate_t, uv_tab, ln2_g, ln2_b):
    t = h1.shape[0]
    tb = PEER_TB
    n_tiles = t // tb
    lag = PEER_SLOTS - 1
    assert t % tb == 0 and LANES % tb == 0
    idx_tiles = eidx_t.T.reshape(n_tiles, tb, N_SEL)
    done = lambda j: jnp.maximum(j - lag, 0)
    const = lambda j: (0, 0)
    kern = functools.partial(_peer_apply_kernel, n_tiles=n_tiles)
    return pl.pallas_call(
        kern,
        grid=(n_tiles + lag,),
        in_specs=[
            pl.BlockSpec(memory_space=pl.ANY),
            pl.BlockSpec((tb, D_MODEL), lambda j: (done(j), 0)),
            pl.BlockSpec((N_SEL, LANES), lambda j: (0, done(j) * tb // LANES)),
            pl.BlockSpec((1, D_MODEL), const),
            pl.BlockSpec((1, D_MODEL), const),
            pl.BlockSpec(memory_space=pl.ANY),
        ],
        out_specs=pl.BlockSpec((tb, D_MODEL), lambda j: (done(j), 0)),
        out_shape=jax.ShapeDtypeStruct((t, D_MODEL), _F32),
        scratch_shapes=[
            pltpu.VMEM((PEER_SLOTS, tb * N_SEL, 2) + ROW_TILE, _F32),
            pltpu.SMEM((2, tb, N_SEL), jnp.int32),
            pltpu.SemaphoreType.DMA((PEER_SLOTS,)),
            pltpu.SemaphoreType.DMA((2,)),
            pltpu.VMEM((SUBLANES * tb, LANES), _F32),
            pltpu.VMEM((N_SEL, tb * LANES), _F32),
            pltpu.VMEM((tb * SUBLANES, LANES), _F32),
        ],
        compiler_params=pltpu.CompilerParams(
            dimension_semantics=("arbitrary",),
            vmem_limit_bytes=int(PEER_SLOTS * tb * N_SEL * 2 * 4096 + 12 * 2 ** 20)),
        name="peer_apply",
    )(idx_tiles, h1, gate_t, ln2_g, ln2_b, uv_tab)


def _rope_tables(pos):
    half = RET_DK // 2
    inv = ROPE_BASE ** (-jnp.arange(half, dtype=_F32) / half)
    ang = jnp.asarray(pos, _F32)[:, None] * inv[None, :]
    cos = jnp.cos(ang)
    sin = jnp.sin(ang)
    return jnp.concatenate([cos, cos], axis=-1), jnp.concatenate([-sin, sin], axis=-1)


def _subkey_blocks(sub_keys):
    z = jnp.zeros((PEER_HEADS, N_KEYS, PEER_DK_HALF), sub_keys.dtype)
    top = jnp.concatenate([sub_keys[:, 0], z], axis=-1)
    bot = jnp.concatenate([z, sub_keys[:, 1]], axis=-1)
    return jnp.concatenate([top, bot], axis=1).astype(_BF16)


def kernel(x_prompt, x_sample, cache_meta_k, cache_meta_v, cache_swa_k, cache_swa_v, state_ret,
           meta_tokens, ln_in_g, ln_in_b, rel_bias, w_in, w_out, attn_sinks, ret_gn_g, ln1_g, ln1_b,
           peer_wq, peer_subkeys, peer_u, peer_v, ln2_g, ln2_b):
    b, s, _ = x_prompt.shape
    db, ds, _ = x_sample.shape
    row = lambda a: a.reshape(1, -1)

    w_in_bf = w_in[0].astype(_BF16)
    w_out_bf = w_out[0].astype(_BF16)
    wq_bf = peer_wq[0].astype(_BF16)
    sk_blk = _subkey_blocks(peer_subkeys[0])
    uv_tab = jnp.stack([peer_u[0].reshape((N_EXPERTS,) + ROW_TILE),
                        peer_v[0].reshape((N_EXPERTS,) + ROW_TILE)], axis=1)
    lnin_g, lnin_b = row(ln_in_g), row(ln_in_b)
    cos_m, sin_m = _rope_tables(np.arange(N_META))
    cos_p, sin_p = _rope_tables(N_META + np.arange(s))
    cos_s, sin_s = _rope_tables(N_META + PAST_LEN + np.arange(ds))
    bias_tab = _rel_bias_tables(rel_bias)
    sinks = attn_sinks[0]

    _, kv_m, rq_m, rk_m, rv_m, _ = _ln_proj(meta_tokens.astype(x_prompt.dtype), N_META, lnin_g, lnin_b,
                                         w_in_bf, cos_m, sin_m, N_META)
    xp2 = x_prompt.reshape(b * s, D_MODEL)
    xs2 = x_sample.reshape(db * ds, D_MODEL)
    q_p, kv_p, rq_p, rk_p, rv_p, rg_p = _ln_proj(xp2, s, lnin_g, lnin_b, w_in_bf, cos_p, sin_p, PROJ_TM)
    q_s, kv_s, rq_s, rk_s, rv_s, rg_s = _ln_proj(xs2, ds, lnin_g, lnin_b, w_in_bf, cos_s, sin_s, ds)

    kv_p3 = kv_p.reshape(b, s, 2 * KV_W)
    o_att_p = _attention_prompt(q_p.reshape(b, s, ATT_W), kv_p3, kv_m, bias_tab, sinks)
    kv_s3 = kv_s.reshape(db, ds, 2 * KV_W)
    o_att_s = _attention_sample(
        q_s.reshape(db, ds, ATT_W), kv_s3,
        cache_swa_k[0].reshape(db, WINDOW, KV_W), cache_swa_v[0].reshape(db, WINDOW, KV_W),
        cache_meta_k[0].reshape(db, N_META, KV_W), cache_meta_v[0].reshape(db, N_META, KV_W),
        bias_tab, sinks)

    zero_state = jnp.zeros((1, RET_HEADS, RET_DK, RET_DK), _F32)
    _, st_meta = _retention(zero_state, rq_m[None], rk_m[None], rv_m[None], N_META)
    st0_p = jnp.broadcast_to(st_meta, (b, RET_HEADS, RET_DK, RET_DK))
    seq3 = lambda a, n, l: a.reshape(n, l, RET_W)
    o_ret_p, st_p = _retention(st0_p, seq3(rq_p, b, s), seq3(rk_p, b, s), seq3(rv_p, b, s), CHUNK)
    o_ret_s, st_s = _retention(state_ret[0].astype(_F32), seq3(rq_s, db, ds), seq3(rk_s, db, ds),
                               seq3(rv_s, db, ds), ds)

    def tail(x2d, o_att, o_ret, rg):
        h1, eidx_t, gate_t = _mix_route(x2d, o_att, o_ret, rg, lnin_g, lnin_b, row(ret_gn_g[0]),
                                        w_out_bf, row(ln1_g[0]), row(ln1_b[0]), wq_bf, sk_blk)
        return _peer_apply(h1, eidx_t, gate_t, uv_tab, row(ln2_g[0]), row(ln2_b[0]))

    y_p = tail(xp2, o_att_p.reshape(b * s, ATT_W), o_ret_p.reshape(b * s, RET_W), rg_p)
    y_s = tail(xs2, o_att_s.reshape(db * ds, ATT_W), o_ret_s.reshape(db * ds, RET_W), rg_s)

    kvh = lambda a, n, l: a.reshape(1, n, l, KV_HEADS, HEAD_DIM)
    meta_k = jnp.broadcast_to(kvh(kv_m[:, :KV_W], 1, N_META), (1, b, N_META, KV_HEADS, HEAD_DIM))
    meta_v = jnp.broadcast_to(kvh(kv_m[:, KV_W:], 1, N_META), (1, b, N_META, KV_HEADS, HEAD_DIM))
    tail_kv = kv_p3[:, s - WINDOW:]
    return (y_p.reshape(b, s, D_MODEL), y_s.reshape(db, ds, D_MODEL),
            meta_k, meta_v,
            kvh(tail_kv[..., :KV_W], b, WINDOW), kvh(tail_kv[..., KV_W:], b, WINDOW),
            st_p[None],
            kvh(kv_s3[..., :KV_W], db, ds), kvh(kv_s3[..., KV_W:], db, ds),
            st_s[None])
```

```python
import functools
import math

import jax
import jax.numpy as jnp
import numpy as np
from jax import lax
from jax.experimental import pallas as pl
from jax.experimental.pallas import tpu as pltpu

D_MODEL = 1024
CHUNK = 64
N_META = 16
PAST_LEN = 2048
ATT_W = 512
HEAD_DIM = 64
N_HEADS = 8
KV_HEADS = 2
GQA_GROUP = N_HEADS // KV_HEADS
KV_W = KV_HEADS * HEAD_DIM
WINDOW = 128
N_BACK = WINDOW // CHUNK
BAND = (N_BACK + 1) * CHUNK
N_KEYS_ATT = BAND + N_META
N_BUCKETS = 32
MAX_DISTANCE = 128
RET_W = 512
RET_HEADS = 4
RET_DK = 128
ROPE_BASE = 10000.0
PROJ_W = 2816
N_KEYS = 128
N_EXPERTS = N_KEYS * N_KEYS
PEER_HEADS = 8
PEER_TOPK = 16
PEER_DK_HALF = 64
N_SEL = PEER_HEADS * PEER_TOPK
ALPHA = 2.0 ** 0.25
LN_EPS = 1e-5
NEG_INF = -1e30

LANES = 128
SUBLANES = 8
ROW_TILE = (SUBLANES, LANES)

PEER_TB = 8
PEER_SLOTS = 3
ROUTE_TM = 256
PROJ_TM = 256

_F32 = jnp.float32
_BF16 = jnp.bfloat16


def _vmem_limit(nbytes):
    return pltpu.CompilerParams(vmem_limit_bytes=int(nbytes))


def _layer_norm(x, g, b):
    mu = jnp.mean(x, axis=-1, keepdims=True)
    xc = x - mu
    var = jnp.mean(xc * xc, axis=-1, keepdims=True)
    return xc * lax.rsqrt(var + LN_EPS) * g + b


def _ln_proj_kernel(x_ref, g_ref, b_ref, w_ref, cos_ref, sin_ref,
                    q_ref, kv_ref, rq_ref, rk_ref, rv_ref, rg_ref):
    h = _layer_norm(x_ref[...], g_ref[...], b_ref[...])
    p = jnp.dot(h.astype(_BF16), w_ref[...], preferred_element_type=_F32)
    cos = cos_ref[...]
    sin = sin_ref[...]

    def rotary(a):
        outs = []
        for hh in range(RET_HEADS):
            seg = a[:, hh * RET_DK:(hh + 1) * RET_DK]
            outs.append(seg * cos + pltpu.roll(seg, RET_DK // 2, axis=1) * sin)
        return jnp.concatenate(outs, axis=-1)

    q_ref[...] = p[:, 0:512]
    kv_ref[...] = p[:, 512:768]
    rq_ref[...] = rotary(p[:, 768:1280])
    rk_ref[...] = rotary(p[:, 1280:1792]) * (RET_DK ** -0.5)
    rv_ref[...] = p[:, 1792:2304]
    rg_ref[...] = p[:, 2304:2816]


def _ln_proj(x2d, seq_len, ln_g, ln_b, w_in_bf, cos, sin, tm):
    t = x2d.shape[0]
    nblk_s = seq_len // tm
    row = lambda i: (i, 0)
    const = lambda i: (0, 0)
    outs = [jax.ShapeDtypeStruct((t, w), _F32) for w in (512, 256, 512, 512, 512, 512)]
    return pl.pallas_call(
        _ln_proj_kernel,
        grid=(t // tm,),
        in_specs=[
            pl.BlockSpec((tm, D_MODEL), row),
            pl.BlockSpec((1, D_MODEL), const),
            pl.BlockSpec((1, D_MODEL), const),
            pl.BlockSpec((D_MODEL, PROJ_W), const),
            pl.BlockSpec((tm, RET_DK), lambda i: (i % nblk_s, 0)),
            pl.BlockSpec((tm, RET_DK), lambda i: (i % nblk_s, 0)),
        ],
        out_specs=[pl.BlockSpec((tm, w), row) for w in (512, 256, 512, 512, 512, 512)],
        out_shape=outs,
        compiler_params=_vmem_limit(48 * 2 ** 20),
        name="ln_proj",
    )(x2d, ln_g, ln_b, w_in_bf, cos, sin)


def _t5_bucket_np(rel):
    nb = N_BUCKETS // 2
    max_exact = nb // 2
    n = np.abs(rel)
    large = max_exact + (np.log(np.maximum(n, max_exact).astype(np.float32) / max_exact)
                         / math.log(MAX_DISTANCE / max_exact) * (nb - max_exact)).astype(np.int32)
    large = np.minimum(large, nb - 1)
    return (np.where(rel > 0, nb, 0) + np.where(n < max_exact, n, large)).astype(np.int32)


def _bias_buckets():
    i = np.arange(CHUNK)
    jb = np.arange(BAND)
    m = np.arange(N_META)
    rel_band = jb[None, :] - N_BACK * CHUNK - i[:, None]
    out = []
    for c in range(3):
        rel_meta = m[None, :] - (N_META + c * CHUNK + i[:, None])
        out.append(np.concatenate([rel_band, rel_meta], axis=-1))
    return _t5_bucket_np(np.stack(out))


def _rel_bias_kernel(rb_ref, bucket_ref, o_ref):
    h = pl.program_id(1)
    bucket = bucket_ref[0]
    acc = jnp.zeros(bucket.shape, _F32)
    for v in range(N_BUCKETS):
        acc = jnp.where(bucket == v, rb_ref[v, h], acc)
    o_ref[0, 0] = acc


def _rel_bias_tables(rel_bias):
    buckets = jnp.asarray(_bias_buckets())
    return pl.pallas_call(
        _rel_bias_kernel,
        grid=(3, N_HEADS),
        in_specs=[
            pl.BlockSpec(memory_space=pltpu.SMEM),
            pl.BlockSpec((1, CHUNK, N_KEYS_ATT), lambda v, h: (v, 0, 0)),
        ],
        out_specs=pl.BlockSpec((1, 1, CHUNK, N_KEYS_ATT), lambda v, h: (v, h, 0, 0)),
        out_shape=jax.ShapeDtypeStruct((3, N_HEADS, CHUNK, N_KEYS_ATT), _F32),
        name="rel_bias",
    )(rel_bias, buckets)


def _attend(q, kk_band, vv_band, kk_meta, vv_meta, bias_ref, sinks_ref, min_valid_col):
    lane = lax.broadcasted_iota(jnp.int32, (1, LANES), 1)
    lo = lane < HEAD_DIM
    kk = jnp.concatenate([kk_band, kk_meta], axis=0)
    vv = jnp.concatenate([vv_band, vv_meta], axis=0)
    kk_r = pltpu.roll(kk, HEAD_DIM, axis=1)
    vv_r = pltpu.roll(vv, HEAD_DIM, axis=1)
    k_dup = [jnp.where(lo, kk, kk_r).astype(_BF16), jnp.where(lo, kk_r, kk).astype(_BF16)]
    v_dup = [jnp.where(lo, vv, vv_r).astype(_BF16), jnp.where(lo, vv_r, vv).astype(_BF16)]
    rows4 = GQA_GROUP * CHUNK
    col = lax.broadcasted_iota(jnp.int32, (rows4, N_KEYS_ATT), 1)
    valid = col >= min_valid_col
    row = lax.broadcasted_iota(jnp.int32, (rows4, 1), 0)
    outs = []
    for kvh in range(KV_HEADS):
        qa = q[:, (2 * kvh) * LANES:(2 * kvh + 1) * LANES]
        qb = q[:, (2 * kvh + 1) * LANES:(2 * kvh + 2) * LANES]
        lhs = jnp.concatenate([jnp.where(lo, qa, 0.0), jnp.where(lo, 0.0, qa),
                               jnp.where(lo, qb, 0.0), jnp.where(lo, 0.0, qb)], axis=0).astype(_BF16)
        heads = [GQA_GROUP * kvh + i for i in range(GQA_GROUP)]
        s = lax.dot_general(lhs, k_dup[kvh], (((1,), (1,)), ((), ())), preferred_element_type=_F32)
        s = s * (HEAD_DIM ** -0.5) + jnp.concatenate([bias_ref[h] for h in heads], axis=0)
        s = jnp.where(valid, s, NEG_INF)
        sink = jnp.where(row < CHUNK, sinks_ref[heads[0]],
                         jnp.where(row < 2 * CHUNK, sinks_ref[heads[1]],
                                   jnp.where(row < 3 * CHUNK, sinks_ref[heads[2]], sinks_ref[heads[3]])))
        m = jnp.maximum(jnp.max(s, axis=-1, keepdims=True), sink)
        p = jnp.exp(s - m)
        denom = jnp.sum(p, axis=-1, keepdims=True) + jnp.exp(sink - m)
        pn = (p / denom).astype(_BF16)
        o4 = jnp.dot(pn, v_dup[kvh], preferred_element_type=_F32)
        outs.append(jnp.where(lo, o4[0:CHUNK], o4[CHUNK:2 * CHUNK]))
        outs.append(jnp.where(lo, o4[2 * CHUNK:3 * CHUNK], o4[3 * CHUNK:]))
    return jnp.concatenate(outs, axis=-1)


def _attn_prompt_kernel(sinks_ref, q_ref, kv2_ref, kv1_ref, kv0_ref, meta_ref, bias_ref, o_ref):
    c = pl.program_id(1)
    kvs = [kv2_ref[0], kv1_ref[0], kv0_ref[0]]
    kk_band = jnp.concatenate([kv[:, :KV_W] for kv in kvs], axis=0)
    vv_band = jnp.concatenate([kv[:, KV_W:] for kv in kvs], axis=0)
    meta = meta_ref[...]
    min_valid = (N_BACK - jnp.minimum(c, N_BACK)) * CHUNK
    o_ref[0] = _attend(q_ref[0], kk_band, vv_band, meta[:, :KV_W], meta[:, KV_W:],
                       bias_ref.at[0], sinks_ref, min_valid)


def _attention_prompt(q, kv, kv_meta, bias_tab, sinks):
    b, s, _ = q.shape
    nc = s // CHUNK

    def back(j):
        return lambda bi, c: (bi, jnp.maximum(c - j, 0), 0)

    return pl.pallas_call(
        _attn_prompt_kernel,
        grid=(b, nc),
        in_specs=[
            pl.BlockSpec(memory_space=pltpu.SMEM),
            pl.BlockSpec((1, CHUNK, ATT_W), lambda bi, c: (bi, c, 0)),
            pl.BlockSpec((1, CHUNK, 2 * KV_W), back(2)),
            pl.BlockSpec((1, CHUNK, 2 * KV_W), back(1)),
            pl.BlockSpec((1, CHUNK, 2 * KV_W), back(0)),
            pl.BlockSpec((N_META, 2 * KV_W), lambda bi, c: (0, 0)),
            pl.BlockSpec((1, N_HEADS, CHUNK, N_KEYS_ATT),
                         lambda bi, c: (jnp.minimum(c, N_BACK), 0, 0, 0)),
        ],
        out_specs=pl.BlockSpec((1, CHUNK, ATT_W), lambda bi, c: (bi, c, 0)),
        out_shape=jax.ShapeDtypeStruct((b, s, ATT_W), _F32),
        name="attn_prompt",
    )(sinks, q, kv, kv, kv, kv_meta, bias_tab)


def _attn_sample_kernel(sinks_ref, q_ref, kv_ref, ck_ref, cv_ref, mk_ref, mv_ref, bias_ref, o_ref):
    kv = kv_ref[0]
    kk_band = jnp.concatenate([ck_ref[0], kv[:, :KV_W]], axis=0)
    vv_band = jnp.concatenate([cv_ref[0], kv[:, KV_W:]], axis=0)
    o_ref[0] = _attend(q_ref[0], kk_band, vv_band, mk_ref[0], mv_ref[0],
                       bias_ref.at[0], sinks_ref, 0)


def _attention_sample(q, kv, cache_k, cache_v, meta_k, meta_v, bias_tab, sinks):
    b = q.shape[0]
    per_b = lambda bi: (bi, 0, 0)
    return pl.pallas_call(
        _attn_sample_kernel,
        grid=(b,),
        in_specs=[
            pl.BlockSpec(memory_space=pltpu.SMEM),
            pl.BlockSpec((1, CHUNK, ATT_W), per_b),
            pl.BlockSpec((1, CHUNK, 2 * KV_W), per_b),
            pl.BlockSpec((1, WINDOW, KV_W), per_b),
            pl.BlockSpec((1, WINDOW, KV_W), per_b),
            pl.BlockSpec((1, N_META, KV_W), per_b),
            pl.BlockSpec((1, N_META, KV_W), per_b),
            pl.BlockSpec((1, N_HEADS, CHUNK, N_KEYS_ATT), lambda bi: (N_BACK, 0, 0, 0)),
        ],
        out_specs=pl.BlockSpec((1, CHUNK, ATT_W), per_b),
        out_shape=jax.ShapeDtypeStruct((b, CHUNK, ATT_W), _F32),
        name="attn_sample",
    )(sinks, q, kv, cache_k, cache_v, meta_k, meta_v, bias_tab)


def _retention_kernel(gl_ref, st0_ref, rq_ref, rk_ref, rv_ref, decay_ref, qdec_ref, kdec_ref,
                      o_ref, st_ref):
    @pl.when(pl.program_id(0) == 0)
    def _():
        st_ref[...] = st0_ref[...]

    nb = rq_ref.shape[0]
    nt_dims = (((1,), (1,)), ((), ()))
    tn_dims = (((0,), (0,)), ((), ()))
    for b in range(nb):
        outs = []
        for h in range(RET_HEADS):
            sl = slice(h * RET_DK, (h + 1) * RET_DK)
            q = rq_ref[b, :, sl]
            k = rk_ref[b, :, sl]
            v = rv_ref[b, :, sl]
            qb = q.astype(_BF16)
            vb = v.astype(_BF16)
            s = lax.dot_general(qb, k.astype(_BF16), nt_dims, preferred_element_type=_F32)
            s = s * decay_ref[h]
            inner = jnp.dot(s.astype(_BF16), vb, preferred_element_type=_F32)
            st = st_ref[b, h]
            cross = jnp.dot(qb, st.astype(_BF16), preferred_element_type=_F32) * qdec_ref[h]
            outs.append(inner + cross)
            kd = (k * kdec_ref[h]).astype(_BF16)
            upd = lax.dot_general(kd, vb, tn_dims, preferred_element_type=_F32)
            st_ref[b, h] = gl_ref[h] * st + upd
        o_ref[b] = jnp.concatenate(outs, axis=-1)


def _retention_tables(chunk_len):
    log_gamma = jnp.log(1.0 - 2.0 ** (-5.0 - jnp.arange(RET_HEADS, dtype=_F32)))
    idx = jnp.arange(chunk_len, dtype=_F32)
    diff = idx[:, None] - idx[None, :]
    decay = jnp.where(diff >= 0, jnp.exp(jnp.maximum(diff, 0.0)[None] * log_gamma[:, None, None]), 0.0)
    q_dec = jnp.exp((idx + 1.0)[None, :] * log_gamma[:, None])
    k_dec = jnp.exp((chunk_len - 1.0 - idx)[None, :] * log_gamma[:, None])
    bcast = lambda t: jnp.broadcast_to(t[:, :, None], (RET_HEADS, chunk_len, RET_DK))
    g_len = jnp.exp(chunk_len * log_gamma)
    return g_len, decay, bcast(q_dec), bcast(k_dec)


def _retention(state0, rq, rk, rv, chunk_len):
    b, s, _ = rq.shape
    g_len, decay, q_dec, k_dec = _retention_tables(chunk_len)
    seq = pl.BlockSpec((b, chunk_len, RET_W), lambda c: (0, c, 0))
    whole4 = pl.BlockSpec((b, RET_HEADS, RET_DK, RET_DK), lambda c: (0, 0, 0, 0))
    tab = lambda n: pl.BlockSpec((RET_HEADS, chunk_len, n), lambda c: (0, 0, 0))
    return pl.pallas_call(
        _retention_kernel,
        grid=(s // chunk_len,),
        in_specs=[pl.BlockSpec(memory_space=pltpu.SMEM), whole4, seq, seq, seq,
                  tab(chunk_len), tab(RET_DK), tab(RET_DK)],
        out_specs=[seq, whole4],
        out_shape=[jax.ShapeDtypeStruct((b, s, RET_W), _F32),
                   jax.ShapeDtypeStruct((b, RET_HEADS, RET_DK, RET_DK), _F32)],
        compiler_params=pltpu.CompilerParams(dimension_semantics=("arbitrary",)),
        name="retention",
    )(g_len, state0, rq, rk, rv, decay, q_dec, k_dec)


def _top16_rows(x, ids=None, sentinel=None):
    if ids is None:
        ids = lax.broadcasted_iota(jnp.int32, x.shape, 0)
        sentinel = x.shape[0]
    vals, idxs = [], []
    for _ in range(PEER_TOPK):
        m = jnp.max(x, axis=0, keepdims=True)
        idx = jnp.min(jnp.where(x == m, ids, sentinel), axis=0, keepdims=True)
        vals.append(m)
        idxs.append(idx)
        x = jnp.where(ids == idx, -jnp.inf, x)
    return vals, idxs


def _candidate_pairs():
    return [(a, b) for a in range(PEER_TOPK) for b in range(PEER_TOPK) if (a + 1) * (b + 1) <= PEER_TOPK]


N_CAND = 56


def _mix_route_kernel(x_ref, oatt_ref, oret_ref, rg_ref, lnin_g_ref, lnin_b_ref, gn_ref, wout_ref,
                      ln1_g_ref, ln1_b_ref, wq_ref, sk_ref, cid_ref,
                      h1_ref, eidx_ref, gate_ref, qs_ref):
    h = _layer_norm(x_ref[...], lnin_g_ref[...], lnin_b_ref[...])
    oret = oret_ref[...]
    rg = rg_ref[...]
    gn = gn_ref[...]
    ys = []
    for hh in range(RET_HEADS):
        sl = slice(hh * RET_DK, (hh + 1) * RET_DK)
        seg = oret[:, sl]
        mu = jnp.mean(seg, axis=-1, keepdims=True)
        sc = seg - mu
        var = jnp.mean(sc * sc, axis=-1, keepdims=True)
        gate = rg[:, sl]
        ys.append(sc * lax.rsqrt(var + LN_EPS) * gn[:, sl] * (gate * jax.nn.sigmoid(gate)))
    mixed = jnp.concatenate([oatt_ref[...]] + ys, axis=-1).astype(_BF16)
    mix = jnp.dot(mixed, wout_ref[...], preferred_element_type=_F32)
    h1 = _layer_norm(ALPHA * h + mix, ln1_g_ref[...], ln1_b_ref[...])
    h1_ref[...] = h1

    qp = jnp.dot(h1.astype(_BF16), wq_ref[...], preferred_element_type=_F32).astype(_BF16)
    for p in range(PEER_HEADS):
        qs_ref[p] = qp[:, p * LANES:(p + 1) * LANES]

    tm = x_ref.shape[0]
    row16 = lax.broadcasted_iota(jnp.int32, (PEER_TOPK, tm), 0)

    def head_body(p, carry):
        sc_t = lax.dot_general(sk_ref[p], qs_ref[p], (((1,), (1,)), ((), ())),
                               preferred_element_type=_F32)
        s1, i1 = _top16_rows(sc_t[:N_KEYS])
        s2, i2 = _top16_rows(sc_t[N_KEYS:])
        i1c = jnp.concatenate(i1, axis=0)
        i2c = jnp.concatenate(i2, axis=0)
        pairs = _candidate_pairs()
        pad = [jnp.full_like(s1[0], -jnp.inf)] * (N_CAND - len(pairs))
        cand = jnp.concatenate([s1[a] + s2[b] for a, b in pairs] + pad, axis=0)
        top, sel = _top16_rows(cand, cid_ref[...], PEER_TOPK * PEER_TOPK)
        e_rows = []
        for r in range(PEER_TOPK):
            a = sel[r] >> 4
            b = sel[r] & (PEER_TOPK - 1)
            e1 = jnp.sum(jnp.where(row16 == a, i1c, 0), axis=0, keepdims=True)
            e2 = jnp.sum(jnp.where(row16 == b, i2c, 0), axis=0, keepdims=True)
            e_rows.append(e1 * N_KEYS + e2)
        topc = jnp.concatenate(top, axis=0)
        ex = jnp.exp(topc - top[0])
        gate = ex / jnp.sum(ex, axis=0, keepdims=True)
        off = pl.multiple_of(p * PEER_TOPK, PEER_TOPK)
        eidx_ref[pl.ds(off, PEER_TOPK), :] = jnp.concatenate(e_rows, axis=0)
        gate_ref[pl.ds(off, PEER_TOPK), :] = gate
        return carry

    lax.fori_loop(0, PEER_HEADS, head_body, 0)


def _mix_route(x2d, o_att, o_ret, rg, lnin_g, lnin_b, gn_g, w_out_bf, ln1_g, ln1_b, wq_bf, sk_blk):
    t = x2d.shape[0]
    tm = ROUTE_TM
    row = lambda i: (i, 0)
    const = lambda i: (0, 0)
    vec = pl.BlockSpec((1, D_MODEL), const)
    assert t % tm == 0
    flat = [a * PEER_TOPK + b for a, b in _candidate_pairs()]
    flat += [PEER_TOPK * PEER_TOPK] * (N_CAND - len(flat))
    cand_ids = jnp.asarray(np.broadcast_to(np.asarray(flat, np.int32)[:, None], (N_CAND, tm)))
    return pl.pallas_call(
        _mix_route_kernel,
        grid=(t // tm,),
        in_specs=[
            pl.BlockSpec((tm, D_MODEL), row),
            pl.BlockSpec((tm, ATT_W), row),
            pl.BlockSpec((tm, RET_W), row),
            pl.BlockSpec((tm, RET_W), row),
            vec, vec,
            pl.BlockSpec((1, RET_W), const),
            pl.BlockSpec((D_MODEL, D_MODEL), const),
            vec, vec,
            pl.BlockSpec((D_MODEL, D_MODEL), const),
            pl.BlockSpec((PEER_HEADS, 2 * N_KEYS, LANES), lambda i: (0, 0, 0)),
            pl.BlockSpec((N_CAND, tm), const),
        ],
        out_specs=[
            pl.BlockSpec((tm, D_MODEL), row),
            pl.BlockSpec((N_SEL, tm), lambda i: (0, i)),
            pl.BlockSpec((N_SEL, tm), lambda i: (0, i)),
        ],
        out_shape=[
            jax.ShapeDtypeStruct((t, D_MODEL), _F32),
            jax.ShapeDtypeStruct((N_SEL, t), jnp.int32),
            jax.ShapeDtypeStruct((N_SEL, t), _F32),
        ],
        scratch_shapes=[pltpu.VMEM((PEER_HEADS, tm, LANES), _BF16)],
        compiler_params=_vmem_limit(40 * 2 ** 20),
        name="mix_route",
    )(x2d, o_att, o_ret, rg, lnin_g, lnin_b, gn_g, w_out_bf, ln1_g, ln1_b, wq_bf, sk_blk, cand_ids)


_BITREV8 = (0, 4, 2, 6, 1, 5, 3, 7)


def _rows_to_sublanes(tiles, sub):
    m4 = sub < 4
    m2 = (sub & 2) == 0
    m1 = (sub & 1) == 0
    t = [tiles[i] for i in _BITREV8]
    c = []
    for a, b in ((0, 1), (2, 3), (4, 5), (6, 7)):
        w = jnp.where(m4, t[a], t[b])
        x = jnp.where(m4, t[b], t[a])
        c.append(w + pltpu.roll(x, 4, axis=0))
    d = []
    for a, b in ((0, 1), (2, 3)):
        d.append(jnp.where(m2, c[a] + pltpu.roll(c[a], 6, axis=0), c[b] + pltpu.roll(c[b], 2, axis=0)))
    return jnp.where(m1, d[0] + pltpu.roll(d[0], 7, axis=0), d[1] + pltpu.roll(d[1], 1, axis=0))


def _peer_apply_kernel(idx_hbm, h1_ref, gate_ref, ln2_g_ref, ln2_b_ref, uv_hbm,
                       y_ref, uvbuf, idx_smem, gsem, isem, xs_ref, wb_ref, o_buf, *, n_tiles):
    tb = PEER_TB
    lag = PEER_SLOTS - 1
    j = pl.program_id(0)

    def idx_copy(tile, buf):
        return pltpu.make_async_copy(idx_hbm.at[tile], idx_smem.at[buf], isem.at[buf])

    @pl.when(j == 0)
    def _():
        idx_copy(0, 0).start()

    issue_on = j < n_tiles
    comp_on = j >= lag
    ibuf = j % 2
    islot = j % PEER_SLOTS
    t = j - lag
    cslot = t % PEER_SLOTS
    lane0 = (t * tb) % LANES

    @pl.when(issue_on)
    def _():
        idx_copy(j, ibuf).wait()

        @pl.when(j + 1 < n_tiles)
        def _():
            idx_copy(j + 1, 1 - ibuf).start()

    @pl.when(comp_on)
    def _():
        pltpu.make_async_copy(uv_hbm.at[pl.ds(0, tb * N_SEL)], uvbuf.at[cslot], gsem.at[cslot]).wait()
        h1 = h1_ref[...]
        for s in range(SUBLANES):
            xs_ref[s * tb:(s + 1) * tb, :] = h1[:, s * LANES:(s + 1) * LANES]

    def token_pass(do_issue, do_comp):
        n_groups = N_SEL // SUBLANES

        def make_issue(tok):
            pending = list(range(N_SEL)) if do_issue else []

            def issue(n):
                for _ in range(min(n, len(pending))):
                    k = pending.pop(0)
                    e = idx_smem[ibuf, tok, k]
                    pltpu.make_async_copy(uv_hbm.at[e], uvbuf.at[islot, tok * N_SEL + k],
                                          gsem.at[islot]).start(priority=k % 2)
            return issue

        if not do_comp:
            def issue_body(tok, carry):
                make_issue(tok)(N_SEL)
                return carry
            lax.fori_loop(0, tb, issue_body, 0)
            return

        issues = [make_issue(tok) for tok in range(tb)]
        sub = lax.broadcasted_iota(jnp.int32, ROW_TILE, 0)

        lane = lax.broadcasted_iota(jnp.int32, (N_SEL, LANES), 1)
        act = jnp.zeros((N_SEL, LANES), _F32)
        for tok in range(tb):
            xt = xs_ref[pl.ds(tok, SUBLANES, stride=tb), :]
            parts = []
            for g in range(n_groups):
                rows = [uvbuf[cslot, tok * N_SEL + g * SUBLANES + r, 0] for r in range(SUBLANES)]
                issues[tok](5)
                prods = [row * xt for row in rows]
                parts.append(jnp.sum(_rows_to_sublanes(prods, sub), axis=1, keepdims=True))
            act = jnp.where(lane == tok, jnp.concatenate(parts, axis=0), act)

        gate0 = pltpu.roll(gate_ref[...], LANES - lane0, axis=1)
        w = jax.nn.gelu(act) * gate0
        for tok in range(tb):
            wb_ref[:, tok * LANES:(tok + 1) * LANES] = jnp.broadcast_to(w[:, tok:tok + 1], (N_SEL, LANES))

        for tok in range(tb):
            accs = [jnp.zeros(ROW_TILE, _F32) for _ in range(4)]
            for k0 in range(0, N_SEL, SUBLANES):
                ks = range(k0, k0 + SUBLANES)
                wks = [jnp.broadcast_to(wb_ref[k:k + 1, tok * LANES:(tok + 1) * LANES], ROW_TILE) for k in ks]
                vks = [uvbuf[cslot, tok * N_SEL + k, 1] for k in ks]
                issues[tok](3)
                for k, wk, vk in zip(ks, wks, vks):
                    accs[k % 4] = accs[k % 4] + wk * vk
            issues[tok](N_SEL)
            o_buf[tok * SUBLANES:(tok + 1) * SUBLANES, :] = (accs[0] + accs[1]) + (accs[2] + accs[3])

    pl.when(jnp.logical_and(issue_on, jnp.logical_not(comp_on)))(lambda: token_pass(True, False))
    pl.when(jnp.logical_and(issue_on, comp_on))(lambda: token_pass(True, True))
    pl.when(jnp.logical_and(jnp.logical_not(issue_on), comp_on))(lambda: token_pass(False, True))

    @pl.when(comp_on)
    def _():
        out = jnp.concatenate([o_buf[pl.ds(s, tb, stride=SUBLANES), :] for s in range(SUBLANES)],
                              axis=-1)
        y_ref[...] = _layer_norm(ALPHA * h1_ref[...] + out, ln2_g_ref[...], ln2_b_ref[...])


def _peer_apply(h1, eidx_t, gate_t, uv_tab, ln2_g, ln2_b):
    t = h1.shape[0]
    tb = PEER_TB
    n_tiles = t // tb
    lag = PEER_SLOTS - 1
    assert t % tb == 0 and LANES % tb == 0
    idx_tiles = eidx_t.T.reshape(n_tiles, tb, N_SEL)
    done = lambda j: jnp.maximum(j - lag, 0)
    const = lambda j: (0, 0)
    kern = functools.partial(_peer_apply_kernel, n_tiles=n_tiles)
    return pl.pallas_call(
        kern,
        grid=(n_tiles + lag,),
        in_specs=[
            pl.BlockSpec(memory_space=pl.ANY),
            pl.BlockSpec((tb, D_MODEL), lambda j: (done(j), 0)),
            pl.BlockSpec((N_SEL, LANES), lambda j: (0, done(j) * tb // LANES)),
            pl.BlockSpec((1, D_MODEL), const),
            pl.BlockSpec((1, D_MODEL), const),
            pl.BlockSpec(memory_space=pl.ANY),
        ],
        out_specs=pl.BlockSpec((tb, D_MODEL), lambda j: (done(j), 0)),
        out_shape=jax.ShapeDtypeStruct((t, D_MODEL), _F32),
        scratch_shapes=[
            pltpu.VMEM((PEER_SLOTS, tb * N_SEL, 2) + ROW_TILE, _F32),
            pltpu.SMEM((2, tb, N_SEL), jnp.int32),
            pltpu.SemaphoreType.DMA((PEER_SLOTS,)),
            pltpu.SemaphoreType.DMA((2,)),
            pltpu.VMEM((SUBLANES * tb, LANES), _F32),
            pltpu.VMEM((N_SEL, tb * LANES), _F32),
            pltpu.VMEM((tb * SUBLANES, LANES), _F32),
        ],
        compiler_params=pltpu.CompilerParams(
            dimension_semantics=("arbitrary",),
            vmem_limit_bytes=int(PEER_SLOTS * tb * N_SEL * 2 * 4096 + 12 * 2 ** 20)),
        name="peer_apply",
    )(idx_tiles, h1, gate_t, ln2_g, ln2_b, uv_tab)


def _rope_tables(pos):
    half = RET_DK // 2
    inv = ROPE_BASE ** (-jnp.arange(half, dtype=_F32) / half)
    ang = jnp.asarray(pos, _F32)[:, None] * inv[None, :]
    cos = jnp.cos(ang)
    sin = jnp.sin(ang)
    return jnp.concatenate([cos, cos], axis=-1), jnp.concatenate([-sin, sin], axis=-1)


def _subkey_blocks(sub_keys):
    z = jnp.zeros((PEER_HEADS, N_KEYS, PEER_DK_HALF), sub_keys.dtype)
    top = jnp.concatenate([sub_keys[:, 0], z], axis=-1)
    bot = jnp.concatenate([z, sub_keys[:, 1]], axis=-1)
    return jnp.concatenate([top, bot], axis=1).astype(_BF16)


def kernel(x_prompt, x_sample, cache_meta_k, cache_meta_v, cache_swa_k, cache_swa_v, state_ret,
           meta_tokens, ln_in_g, ln_in_b, rel_bias, w_in, w_out, attn_sinks, ret_gn_g, ln1_g, ln1_b,
           peer_wq, peer_subkeys, peer_u, peer_v, ln2_g, ln2_b):
    b, s, _ = x_prompt.shape
    db, ds, _ = x_sample.shape
    row = lambda a: a.reshape(1, -1)

    w_in_bf = w_in[0].astype(_BF16)
    w_out_bf = w_out[0].astype(_BF16)
    wq_bf = peer_wq[0].astype(_BF16)
    sk_blk = _subkey_blocks(peer_subkeys[0])
    uv_tab = jnp.concatenate([peer_u[0], peer_v[0]], axis=1).reshape((N_EXPERTS, 2) + ROW_TILE)
    lnin_g, lnin_b = row(ln_in_g), row(ln_in_b)
    cos_m, sin_m = _rope_tables(np.arange(N_META))
    cos_p, sin_p = _rope_tables(N_META + np.arange(s))
    cos_s, sin_s = _rope_tables(N_META + PAST_LEN + np.arange(ds))
    bias_tab = _rel_bias_tables(rel_bias)
    sinks = attn_sinks[0]

    _, kv_m, rq_m, rk_m, rv_m, _ = _ln_proj(meta_tokens.astype(x_prompt.dtype), N_META, lnin_g, lnin_b,
                                         w_in_bf, cos_m, sin_m, N_META)
    xp2 = x_prompt.reshape(b * s, D_MODEL)
    xs2 = x_sample.reshape(db * ds, D_MODEL)
    q_p, kv_p, rq_p, rk_p, rv_p, rg_p = _ln_proj(xp2, s, lnin_g, lnin_b, w_in_bf, cos_p, sin_p, PROJ_TM)
    q_s, kv_s, rq_s, rk_s, rv_s, rg_s = _ln_proj(xs2, ds, lnin_g, lnin_b, w_in_bf, cos_s, sin_s, ds)

    kv_p3 = kv_p.reshape(b, s, 2 * KV_W)
    o_att_p = _attention_prompt(q_p.reshape(b, s, ATT_W), kv_p3, kv_m, bias_tab, sinks)
    kv_s3 = kv_s.reshape(db, ds, 2 * KV_W)
    o_att_s = _attention_sample(
        q_s.reshape(db, ds, ATT_W), kv_s3,
        cache_swa_k[0].reshape(db, WINDOW, KV_W), cache_swa_v[0].reshape(db, WINDOW, KV_W),
        cache_meta_k[0].reshape(db, N_META, KV_W), cache_meta_v[0].reshape(db, N_META, KV_W),
        bias_tab, sinks)

    zero_state = jnp.zeros((1, RET_HEADS, RET_DK, RET_DK), _F32)
    _, st_meta = _retention(zero_state, rq_m[None], rk_m[None], rv_m[None], N_META)
    st0_p = jnp.broadcast_to(st_meta, (b, RET_HEADS, RET_DK, RET_DK))
    seq3 = lambda a, n, l: a.reshape(n, l, RET_W)
    o_ret_p, st_p = _retention(st0_p, seq3(rq_p, b, s), seq3(rk_p, b, s), seq3(rv_p, b, s), CHUNK)
    o_ret_s, st_s = _retention(state_ret[0].astype(_F32), seq3(rq_s, db, ds), seq3(rk_s, db, ds),
                               seq3(rv_s, db, ds), ds)

    def tail(x2d, o_att, o_ret, rg):
        h1, eidx_t, gate_t = _mix_route(x2d, o_att, o_ret, rg, lnin_g, lnin_b, row(ret_gn_g[0]),
                                        w_out_bf, row(ln1_g[0]), row(ln1_b[0]), wq_bf, sk_blk)
        return _peer_apply(h1, eidx_t, gate_t, uv_tab, row(ln2_g[0]), row(ln2_b[0]))

    y_p = tail(xp2, o_att_p.reshape(b * s, ATT_W), o_ret_p.reshape(b * s, RET_W), rg_p)
    y_s = tail(xs2, o_att_s.reshape(db * ds, ATT_W), o_ret_s.reshape(db * ds, RET_W), rg_s)

    kvh = lambda a, n, l: a.reshape(1, n, l, KV_HEADS, HEAD_DIM)
    meta_k = jnp.broadcast_to(kvh(kv_m[:, :KV_W], 1, N_META), (1, b, N_META, KV_HEADS, HEAD_DIM))
    meta_v = jnp.broadcast_to(kvh(kv_m[:, KV_W:], 1, N_META), (1, b, N_META, KV_HEADS, HEAD_DIM))
    tail_kv = kv_p3[:, s - WINDOW:]
    return (y_p.reshape(b, s, D_MODEL), y_s.reshape(db, ds, D_MODEL),
            meta_k, meta_v,
            kvh(tail_kv[..., :KV_W], b, WINDOW), kvh(tail_kv[..., KV_W:], b, WINDOW),
            st_p[None],
            kvh(kv_s3[..., :KV_W], db, ds), kvh(kv_s3[..., KV_W:], db, ds),
            st_s[None])
```

```python
import functools
import math

import jax
import jax.numpy as jnp
import numpy as np
from jax import lax
from jax.experimental import pallas as pl
from jax.experimental.pallas import tpu as pltpu

D_MODEL = 1024
CHUNK = 64
N_META = 16
PAST_LEN = 2048
ATT_W = 512
HEAD_DIM = 64
N_HEADS = 8
KV_HEADS = 2
GQA_GROUP = N_HEADS // KV_HEADS
KV_W = KV_HEADS * HEAD_DIM
WINDOW = 128
N_BACK = WINDOW // CHUNK
BAND = (N_BACK + 1) * CHUNK
N_KEYS_ATT = BAND + N_META
N_BUCKETS = 32
MAX_DISTANCE = 128
RET_W = 512
RET_HEADS = 4
RET_DK = 128
ROPE_BASE = 10000.0
PROJ_W = 2816
N_KEYS = 128
N_EXPERTS = N_KEYS * N_KEYS
PEER_HEADS = 8
PEER_TOPK = 16
PEER_DK_HALF = 64
N_SEL = PEER_HEADS * PEER_TOPK
ALPHA = 2.0 ** 0.25
LN_EPS = 1e-5
NEG_INF = -1e30

LANES = 128
SUBLANES = 8
ROW_TILE = (SUBLANES, LANES)

PEER_TB = 8
PEER_NT = 2
PEER_SLOTS = 3
PEER_REGION = 4
PEER_ISSUE_U = 5
ROUTE_TM = 256
PROJ_TM = 256

_F32 = jnp.float32
_BF16 = jnp.bfloat16


def _vmem_limit(nbytes):
    return pltpu.CompilerParams(vmem_limit_bytes=int(nbytes))


def _layer_norm(x, g, b):
    mu = jnp.mean(x, axis=-1, keepdims=True)
    xc = x - mu
    var = jnp.mean(xc * xc, axis=-1, keepdims=True)
    return xc * lax.rsqrt(var + LN_EPS) * g + b


def _ln_proj_kernel(x_ref, g_ref, b_ref, w_ref, cos_ref, sin_ref,
                    q_ref, kv_ref, rq_ref, rk_ref, rv_ref, rg_ref):
    h = _layer_norm(x_ref[...], g_ref[...], b_ref[...])
    p = jnp.dot(h.astype(_BF16), w_ref[...], preferred_element_type=_F32)
    cos = cos_ref[...]
    sin = sin_ref[...]

    def rotary(a):
        outs = []
        for hh in range(RET_HEADS):
            seg = a[:, hh * RET_DK:(hh + 1) * RET_DK]
            outs.append(seg * cos + pltpu.roll(seg, RET_DK // 2, axis=1) * sin)
        return jnp.concatenate(outs, axis=-1)

    q_ref[...] = p[:, 0:512]
    kv_ref[...] = p[:, 512:768]
    rq_ref[...] = rotary(p[:, 768:1280])
    rk_ref[...] = rotary(p[:, 1280:1792]) * (RET_DK ** -0.5)
    rv_ref[...] = p[:, 1792:2304]
    rg_ref[...] = p[:, 2304:2816]


def _ln_proj(x2d, seq_len, ln_g, ln_b, w_in_bf, cos, sin, tm):
    t = x2d.shape[0]
    nblk_s = seq_len // tm
    row = lambda i: (i, 0)
    const = lambda i: (0, 0)
    outs = [jax.ShapeDtypeStruct((t, w), _F32) for w in (512, 256, 512, 512, 512, 512)]
    return pl.pallas_call(
        _ln_proj_kernel,
        grid=(t // tm,),
        in_specs=[
            pl.BlockSpec((tm, D_MODEL), row),
            pl.BlockSpec((1, D_MODEL), const),
            pl.BlockSpec((1, D_MODEL), const),
            pl.BlockSpec((D_MODEL, PROJ_W), const),
            pl.BlockSpec((tm, RET_DK), lambda i: (i % nblk_s, 0)),
            pl.BlockSpec((tm, RET_DK), lambda i: (i % nblk_s, 0)),
        ],
        out_specs=[pl.BlockSpec((tm, w), row) for w in (512, 256, 512, 512, 512, 512)],
        out_shape=outs,
        compiler_params=_vmem_limit(48 * 2 ** 20),
        name="ln_proj",
    )(x2d, ln_g, ln_b, w_in_bf, cos, sin)


def _t5_bucket_np(rel):
    nb = N_BUCKETS // 2
    max_exact = nb // 2
    n = np.abs(rel)
    large = max_exact + (np.log(np.maximum(n, max_exact).astype(np.float32) / max_exact)
                         / math.log(MAX_DISTANCE / max_exact) * (nb - max_exact)).astype(np.int32)
    large = np.minimum(large, nb - 1)
    return (np.where(rel > 0, nb, 0) + np.where(n < max_exact, n, large)).astype(np.int32)


def _bias_buckets():
    i = np.arange(CHUNK)
    jb = np.arange(BAND)
    m = np.arange(N_META)
    rel_band = jb[None, :] - N_BACK * CHUNK - i[:, None]
    out = []
    for c in range(3):
        rel_meta = m[None, :] - (N_META + c * CHUNK + i[:, None])
        out.append(np.concatenate([rel_band, rel_meta], axis=-1))
    return _t5_bucket_np(np.stack(out))


def _rel_bias_kernel(rb_ref, bucket_ref, o_ref):
    h = pl.program_id(1)
    bucket = bucket_ref[0]
    acc = jnp.zeros(bucket.shape, _F32)
    for v in range(N_BUCKETS):
        acc = jnp.where(bucket == v, rb_ref[v, h], acc)
    o_ref[0, 0] = acc


def _rel_bias_tables(rel_bias):
    buckets = jnp.asarray(_bias_buckets())
    return pl.pallas_call(
        _rel_bias_kernel,
        grid=(3, N_HEADS),
        in_specs=[
            pl.BlockSpec(memory_space=pltpu.SMEM),
            pl.BlockSpec((1, CHUNK, N_KEYS_ATT), lambda v, h: (v, 0, 0)),
        ],
        out_specs=pl.BlockSpec((1, 1, CHUNK, N_KEYS_ATT), lambda v, h: (v, h, 0, 0)),
        out_shape=jax.ShapeDtypeStruct((3, N_HEADS, CHUNK, N_KEYS_ATT), _F32),
        name="rel_bias",
    )(rel_bias, buckets)


def _attend(q, kk_band, vv_band, kk_meta, vv_meta, bias_ref, sinks_ref, min_valid_col):
    lane = lax.broadcasted_iota(jnp.int32, (1, LANES), 1)
    lo = lane < HEAD_DIM
    kk = jnp.concatenate([kk_band, kk_meta], axis=0)
    vv = jnp.concatenate([vv_band, vv_meta], axis=0)
    kk_r = pltpu.roll(kk, HEAD_DIM, axis=1)
    vv_r = pltpu.roll(vv, HEAD_DIM, axis=1)
    k_dup = [jnp.where(lo, kk, kk_r).astype(_BF16), jnp.where(lo, kk_r, kk).astype(_BF16)]
    v_dup = [jnp.where(lo, vv, vv_r).astype(_BF16), jnp.where(lo, vv_r, vv).astype(_BF16)]
    rows4 = GQA_GROUP * CHUNK
    col = lax.broadcasted_iota(jnp.int32, (rows4, N_KEYS_ATT), 1)
    valid = col >= min_valid_col
    row = lax.broadcasted_iota(jnp.int32, (rows4, 1), 0)
    outs = []
    for kvh in range(KV_HEADS):
        qa = q[:, (2 * kvh) * LANES:(2 * kvh + 1) * LANES]
        qb = q[:, (2 * kvh + 1) * LANES:(2 * kvh + 2) * LANES]
        lhs = jnp.concatenate([jnp.where(lo, qa, 0.0), jnp.where(lo, 0.0, qa),
                               jnp.where(lo, qb, 0.0), jnp.where(lo, 0.0, qb)], axis=0).astype(_BF16)
        heads = [GQA_GROUP * kvh + i for i in range(GQA_GROUP)]
        s = lax.dot_general(lhs, k_dup[kvh], (((1,), (1,)), ((), ())), preferred_element_type=_F32)
        s = s * (HEAD_DIM ** -0.5) + jnp.concatenate([bias_ref[h] for h in heads], axis=0)
        s = jnp.where(valid, s, NEG_INF)
        sink = jnp.where(row < CHUNK, sinks_ref[heads[0]],
                         jnp.where(row < 2 * CHUNK, sinks_ref[heads[1]],
                                   jnp.where(row < 3 * CHUNK, sinks_ref[heads[2]], sinks_ref[heads[3]])))
        m = jnp.maximum(jnp.max(s, axis=-1, keepdims=True), sink)
        p = jnp.exp(s - m)
        denom = jnp.sum(p, axis=-1, keepdims=True) + jnp.exp(sink - m)
        pn = (p / denom).astype(_BF16)
        o4 = jnp.dot(pn, v_dup[kvh], preferred_element_type=_F32)
        outs.append(jnp.where(lo, o4[0:CHUNK], o4[CHUNK:2 * CHUNK]))
        outs.append(jnp.where(lo, o4[2 * CHUNK:3 * CHUNK], o4[3 * CHUNK:]))
    return jnp.concatenate(outs, axis=-1)


def _attn_prompt_kernel(sinks_ref, q_ref, kv2_ref, kv1_ref, kv0_ref, meta_ref, bias_ref, o_ref):
    c = pl.program_id(1)
    kvs = [kv2_ref[0], kv1_ref[0], kv0_ref[0]]
    kk_band = jnp.concatenate([kv[:, :KV_W] for kv in kvs], axis=0)
    vv_band = jnp.concatenate([kv[:, KV_W:] for kv in kvs], axis=0)
    meta = meta_ref[...]
    min_valid = (N_BACK - jnp.minimum(c, N_BACK)) * CHUNK
    o_ref[0] = _attend(q_ref[0], kk_band, vv_band, meta[:, :KV_W], meta[:, KV_W:],
                       bias_ref.at[0], sinks_ref, min_valid)


def _attention_prompt(q, kv, kv_meta, bias_tab, sinks):
    b, s, _ = q.shape
    nc = s // CHUNK

    def back(j):
        return lambda bi, c: (bi, jnp.maximum(c - j, 0), 0)

    return pl.pallas_call(
        _attn_prompt_kernel,
        grid=(b, nc),
        in_specs=[
            pl.BlockSpec(memory_space=pltpu.SMEM),
            pl.BlockSpec((1, CHUNK, ATT_W), lambda bi, c: (bi, c, 0)),
            pl.BlockSpec((1, CHUNK, 2 * KV_W), back(2)),
            pl.BlockSpec((1, CHUNK, 2 * KV_W), back(1)),
            pl.BlockSpec((1, CHUNK, 2 * KV_W), back(0)),
            pl.BlockSpec((N_META, 2 * KV_W), lambda bi, c: (0, 0)),
            pl.BlockSpec((1, N_HEADS, CHUNK, N_KEYS_ATT),
                         lambda bi, c: (jnp.minimum(c, N_BACK), 0, 0, 0)),
        ],
        out_specs=pl.BlockSpec((1, CHUNK, ATT_W), lambda bi, c: (bi, c, 0)),
        out_shape=jax.ShapeDtypeStruct((b, s, ATT_W), _F32),
        name="attn_prompt",
    )(sinks, q, kv, kv, kv, kv_meta, bias_tab)


def _attn_sample_kernel(sinks_ref, q_ref, kv_ref, ck_ref, cv_ref, mk_ref, mv_ref, bias_ref, o_ref):
    kv = kv_ref[0]
    kk_band = jnp.concatenate([ck_ref[0], kv[:, :KV_W]], axis=0)
    vv_band = jnp.concatenate([cv_ref[0], kv[:, KV_W:]], axis=0)
    o_ref[0] = _attend(q_ref[0], kk_band, vv_band, mk_ref[0], mv_ref[0],
                       bias_ref.at[0], sinks_ref, 0)


def _attention_sample(q, kv, cache_k, cache_v, meta_k, meta_v, bias_tab, sinks):
    b = q.shape[0]
    per_b = lambda bi: (bi, 0, 0)
    return pl.pallas_call(
        _attn_sample_kernel,
        grid=(b,),
        in_specs=[
            pl.BlockSpec(memory_space=pltpu.SMEM),
            pl.BlockSpec((1, CHUNK, ATT_W), per_b),
            pl.BlockSpec((1, CHUNK, 2 * KV_W), per_b),
            pl.BlockSpec((1, WINDOW, KV_W), per_b),
            pl.BlockSpec((1, WINDOW, KV_W), per_b),
            pl.BlockSpec((1, N_META, KV_W), per_b),
            pl.BlockSpec((1, N_META, KV_W), per_b),
            pl.BlockSpec((1, N_HEADS, CHUNK, N_KEYS_ATT), lambda bi: (N_BACK, 0, 0, 0)),
        ],
        out_specs=pl.BlockSpec((1, CHUNK, ATT_W), per_b),
        out_shape=jax.ShapeDtypeStruct((b, CHUNK, ATT_W), _F32),
        name="attn_sample",
    )(sinks, q, kv, cache_k, cache_v, meta_k, meta_v, bias_tab)


def _retention_kernel(gl_ref, st0_ref, rq_ref, rk_ref, rv_ref, decay_ref, qdec_ref, kdec_ref,
                      o_ref, st_ref):
    @pl.when(pl.program_id(0) == 0)
    def _():
        st_ref[...] = st0_ref[...]

    nb = rq_ref.shape[0]
    nt_dims = (((1,), (1,)), ((), ()))
    tn_dims = (((0,), (0,)), ((), ()))
    for b in range(nb):
        outs = []
        for h in range(RET_HEADS):
            sl = slice(h * RET_DK, (h + 1) * RET_DK)
            q = rq_ref[b, :, sl]
            k = rk_ref[b, :, sl]
            v = rv_ref[b, :, sl]
            qb = q.astype(_BF16)
            vb = v.astype(_BF16)
            s = lax.dot_general(qb, k.astype(_BF16), nt_dims, preferred_element_type=_F32)
            s = s * decay_ref[h]
            inner = jnp.dot(s.astype(_BF16), vb, preferred_element_type=_F32)
            st = st_ref[b, h]
            cross = jnp.dot(qb, st.astype(_BF16), preferred_element_type=_F32) * qdec_ref[h]
            outs.append(inner + cross)
            kd = (k * kdec_ref[h]).astype(_BF16)
            upd = lax.dot_general(kd, vb, tn_dims, preferred_element_type=_F32)
            st_ref[b, h] = gl_ref[h] * st + upd
        o_ref[b] = jnp.concatenate(outs, axis=-1)


def _retention_tables(chunk_len):
    log_gamma = jnp.log(1.0 - 2.0 ** (-5.0 - jnp.arange(RET_HEADS, dtype=_F32)))
    idx = jnp.arange(chunk_len, dtype=_F32)
    diff = idx[:, None] - idx[None, :]
    decay = jnp.where(diff >= 0, jnp.exp(jnp.maximum(diff, 0.0)[None] * log_gamma[:, None, None]), 0.0)
    q_dec = jnp.exp((idx + 1.0)[None, :] * log_gamma[:, None])
    k_dec = jnp.exp((chunk_len - 1.0 - idx)[None, :] * log_gamma[:, None])
    bcast = lambda t: jnp.broadcast_to(t[:, :, None], (RET_HEADS, chunk_len, RET_DK))
    g_len = jnp.exp(chunk_len * log_gamma)
    return g_len, decay, bcast(q_dec), bcast(k_dec)


def _retention(state0, rq, rk, rv, chunk_len):
    b, s, _ = rq.shape
    g_len, decay, q_dec, k_dec = _retention_tables(chunk_len)
    seq = pl.BlockSpec((b, chunk_len, RET_W), lambda c: (0, c, 0))
    whole4 = pl.BlockSpec((b, RET_HEADS, RET_DK, RET_DK), lambda c: (0, 0, 0, 0))
    tab = lambda n: pl.BlockSpec((RET_HEADS, chunk_len, n), lambda c: (0, 0, 0))
    return pl.pallas_call(
        _retention_kernel,
        grid=(s // chunk_len,),
        in_specs=[pl.BlockSpec(memory_space=pltpu.SMEM), whole4, seq, seq, seq,
                  tab(chunk_len), tab(RET_DK), tab(RET_DK)],
        out_specs=[seq, whole4],
        out_shape=[jax.ShapeDtypeStruct((b, s, RET_W), _F32),
                   jax.ShapeDtypeStruct((b, RET_HEADS, RET_DK, RET_DK), _F32)],
        compiler_params=pltpu.CompilerParams(dimension_semantics=("arbitrary",)),
        name="retention",
    )(g_len, state0, rq, rk, rv, decay, q_dec, k_dec)


def _top16_rows(x, ids=None, sentinel=None):
    if ids is None:
        ids = lax.broadcasted_iota(jnp.int32, x.shape, 0)
        sentinel = x.shape[0]
    vals, idxs = [], []
    for _ in range(PEER_TOPK):
        m = jnp.max(x, axis=0, keepdims=True)
        idx = jnp.min(jnp.where(x == m, ids, sentinel), axis=0, keepdims=True)
        vals.append(m)
        idxs.append(idx)
        x = jnp.where(ids == idx, -jnp.inf, x)
    return vals, idxs


def _candidate_pairs():
    return [(a, b) for a in range(PEER_TOPK) for b in range(PEER_TOPK) if (a + 1) * (b + 1) <= PEER_TOPK]


N_CAND = 56


def _mix_route_kernel(x_ref, oatt_ref, oret_ref, rg_ref, lnin_g_ref, lnin_b_ref, gn_ref, wout_ref,
                      ln1_g_ref, ln1_b_ref, wq_ref, sk_ref, cid_ref,
                      h1_ref, eidx_ref, gate_ref, qs_ref):
    h = _layer_norm(x_ref[...], lnin_g_ref[...], lnin_b_ref[...])
    oret = oret_ref[...]
    rg = rg_ref[...]
    gn = gn_ref[...]
    ys = []
    for hh in range(RET_HEADS):
        sl = slice(hh * RET_DK, (hh + 1) * RET_DK)
        seg = oret[:, sl]
        mu = jnp.mean(seg, axis=-1, keepdims=True)
        sc = seg - mu
        var = jnp.mean(sc * sc, axis=-1, keepdims=True)
        gate = rg[:, sl]
        ys.append(sc * lax.rsqrt(var + LN_EPS) * gn[:, sl] * (gate * jax.nn.sigmoid(gate)))
    mixed = jnp.concatenate([oatt_ref[...]] + ys, axis=-1).astype(_BF16)
    mix = jnp.dot(mixed, wout_ref[...], preferred_element_type=_F32)
    h1 = _layer_norm(ALPHA * h + mix, ln1_g_ref[...], ln1_b_ref[...])
    h1_ref[...] = h1

    qp = jnp.dot(h1.astype(_BF16), wq_ref[...], preferred_element_type=_F32).astype(_BF16)
    for p in range(PEER_HEADS):
        qs_ref[p] = qp[:, p * LANES:(p + 1) * LANES]

    tm = x_ref.shape[0]
    row16 = lax.broadcasted_iota(jnp.int32, (PEER_TOPK, tm), 0)

    def head_body(p, carry):
        sc_t = lax.dot_general(sk_ref[p], qs_ref[p], (((1,), (1,)), ((), ())),
                               preferred_element_type=_F32)
        s1, i1 = _top16_rows(sc_t[:N_KEYS])
        s2, i2 = _top16_rows(sc_t[N_KEYS:])
        i1c = jnp.concatenate(i1, axis=0)
        i2c = jnp.concatenate(i2, axis=0)
        pairs = _candidate_pairs()
        pad = [jnp.full_like(s1[0], -jnp.inf)] * (N_CAND - len(pairs))
        cand = jnp.concatenate([s1[a] + s2[b] for a, b in pairs] + pad, axis=0)
        top, sel = _top16_rows(cand, cid_ref[...], PEER_TOPK * PEER_TOPK)
        e_rows = []
        for r in range(PEER_TOPK):
            a = sel[r] >> 4
            b = sel[r] & (PEER_TOPK - 1)
            e1 = jnp.sum(jnp.where(row16 == a, i1c, 0), axis=0, keepdims=True)
            e2 = jnp.sum(jnp.where(row16 == b, i2c, 0), axis=0, keepdims=True)
            e_rows.append(e1 * N_KEYS + e2)
        topc = jnp.concatenate(top, axis=0)
        ex = jnp.exp(topc - top[0])
        gate = ex / jnp.sum(ex, axis=0, keepdims=True)
        off = pl.multiple_of(p * PEER_TOPK, PEER_TOPK)
        eidx_ref[pl.ds(off, PEER_TOPK), :] = jnp.concatenate(e_rows, axis=0)
        gate_ref[pl.ds(off, PEER_TOPK), :] = gate
        return carry

    lax.fori_loop(0, PEER_HEADS, head_body, 0)


def _mix_route(x2d, o_att, o_ret, rg, lnin_g, lnin_b, gn_g, w_out_bf, ln1_g, ln1_b, wq_bf, sk_blk):
    t = x2d.shape[0]
    tm = ROUTE_TM
    row = lambda i: (i, 0)
    const = lambda i: (0, 0)
    vec = pl.BlockSpec((1, D_MODEL), const)
    assert t % tm == 0
    flat = [a * PEER_TOPK + b for a, b in _candidate_pairs()]
    flat += [PEER_TOPK * PEER_TOPK] * (N_CAND - len(flat))
    cand_ids = jnp.asarray(np.broadcast_to(np.asarray(flat, np.int32)[:, None], (N_CAND, tm)))
    return pl.pallas_call(
        _mix_route_kernel,
        grid=(t // tm,),
        in_specs=[
            pl.BlockSpec((tm, D_MODEL), row),
            pl.BlockSpec((tm, ATT_W), row),
            pl.BlockSpec((tm, RET_W), row),
            pl.BlockSpec((tm, RET_W), row),
            vec, vec,
            pl.BlockSpec((1, RET_W), const),
            pl.BlockSpec((D_MODEL, D_MODEL), const),
            vec, vec,
            pl.BlockSpec((D_MODEL, D_MODEL), const),
            pl.BlockSpec((PEER_HEADS, 2 * N_KEYS, LANES), lambda i: (0, 0, 0)),
            pl.BlockSpec((N_CAND, tm), const),
        ],
        out_specs=[
            pl.BlockSpec((tm, D_MODEL), row),
            pl.BlockSpec((N_SEL, tm), lambda i: (0, i)),
            pl.BlockSpec((N_SEL, tm), lambda i: (0, i)),
        ],
        out_shape=[
            jax.ShapeDtypeStruct((t, D_MODEL), _F32),
            jax.ShapeDtypeStruct((N_SEL, t), jnp.int32),
            jax.ShapeDtypeStruct((N_SEL, t), _F32),
        ],
        scratch_shapes=[pltpu.VMEM((PEER_HEADS, tm, LANES), _BF16)],
        compiler_params=_vmem_limit(40 * 2 ** 20),
        name="mix_route",
    )(x2d, o_att, o_ret, rg, lnin_g, lnin_b, gn_g, w_out_bf, ln1_g, ln1_b, wq_bf, sk_blk, cand_ids)


_BITREV8 = (0, 4, 2, 6, 1, 5, 3, 7)


def _rows_to_sublanes(tiles, sub):
    m4 = sub < 4
    m2 = (sub & 2) == 0
    m1 = (sub & 1) == 0
    t = [tiles[i] for i in _BITREV8]
    c = []
    for a, b in ((0, 1), (2, 3), (4, 5), (6, 7)):
        w = jnp.where(m4, t[a], t[b])
        x = jnp.where(m4, t[b], t[a])
        c.append(w + pltpu.roll(x, 4, axis=0))
    d = []
    for a, b in ((0, 1), (2, 3)):
        d.append(jnp.where(m2, c[a] + pltpu.roll(c[a], 6, axis=0), c[b] + pltpu.roll(c[b], 2, axis=0)))
    return jnp.where(m1, d[0] + pltpu.roll(d[0], 7, axis=0), d[1] + pltpu.roll(d[1], 1, axis=0))


def _peer_apply_kernel(idx_hbm, h1_ref, gate_ref, ln2_g_ref, ln2_b_ref, uv_hbm,
                       y_ref, uvbuf, idx_smem, gsem, isem, xs_ref, wb_ref, o_buf, *, n_tiles):
    tb, nt = PEER_TB, PEER_NT
    step_tok = tb * nt
    n_steps = n_tiles // nt
    j = pl.program_id(0)
    issue_on = j < n_steps
    comp_on = j >= 1
    lane0 = ((j - 1) * step_tok) % LANES

    def idx_copy(tile, buf):
        return pltpu.make_async_copy(idx_hbm.at[tile], idx_smem.at[buf], isem.at[buf])

    @pl.when(j == 0)
    def _():
        for i in range(nt):
            idx_copy(i, i).start()

    @pl.when(comp_on)
    def _():
        h1 = h1_ref[...]
        for s in range(SUBLANES):
            xs_ref[s * step_tok:(s + 1) * step_tok, :] = h1[:, s * LANES:(s + 1) * LANES]

    def tile_pass(i, do_issue, do_comp):
        n_groups = N_SEL // SUBLANES
        tile_new = nt * j + i
        islot = tile_new % PEER_SLOTS
        cslot = (tile_new - nt) % PEER_SLOTS
        if do_issue:
            idx_copy(tile_new, i).wait()
        if do_comp:
            pltpu.make_async_copy(uv_hbm.at[pl.ds(0, tb * N_SEL)], uvbuf.at[cslot], gsem.at[cslot]).wait()

        def make_issue(tok):
            pending = list(range(N_SEL)) if do_issue else []

            def issue(n):
                for _ in range(min(n, len(pending))):
                    k = pending.pop(0)
                    e = idx_smem[i, tok, k]
                    pltpu.make_async_copy(uv_hbm.at[e], uvbuf.at[islot, tok * N_SEL + k],
                                          gsem.at[islot]).start(priority=k % 2)
            return issue

        def refill():
            @pl.when(tile_new + nt < n_tiles)
            def _():
                idx_copy(tile_new + nt, i).start()

        if not do_comp:
            def issue_body(tok, carry):
                make_issue(tok)(N_SEL)
                return carry
            lax.fori_loop(0, tb, issue_body, 0)
            refill()
            return

        issues = [make_issue(tok) for tok in range(tb)]
        sub = lax.broadcasted_iota(jnp.int32, ROW_TILE, 0)

        lane = lax.broadcasted_iota(jnp.int32, (N_SEL, LANES), 1)
        act = jnp.zeros((N_SEL, LANES), _F32)
        for tok in range(tb):
            xt = xs_ref[pl.ds(i * tb + tok, SUBLANES, stride=step_tok), :]
            parts = []

            def load_u(g):
                return [uvbuf[cslot, tok * N_SEL + g * SUBLANES + r, 0] for r in range(SUBLANES)]

            for g0 in range(0, n_groups, PEER_REGION):
                region = [load_u(g) for g in range(g0, g0 + PEER_REGION)]
                issues[tok](PEER_ISSUE_U * PEER_REGION)
                for rows in region:
                    prods = [row * xt for row in rows]
                    parts.append(jnp.sum(_rows_to_sublanes(prods, sub), axis=1, keepdims=True))
            act = jnp.where(lane == tok, jnp.concatenate(parts, axis=0), act)

        gate0 = pltpu.roll(gate_ref[...], (2 * LANES - lane0 - i * tb) % LANES, axis=1)
        w = jax.nn.gelu(act) * gate0
        for tok in range(tb):
            wb_ref[:, tok * LANES:(tok + 1) * LANES] = jnp.broadcast_to(w[:, tok:tok + 1], (N_SEL, LANES))

        for tok in range(tb):
            accs = [jnp.zeros(ROW_TILE, _F32) for _ in range(4)]

            def load_v(k0):
                ks = range(k0, k0 + SUBLANES)
                return ([jnp.broadcast_to(wb_ref[k:k + 1, tok * LANES:(tok + 1) * LANES], ROW_TILE) for k in ks],
                        [uvbuf[cslot, tok * N_SEL + k, 1] for k in ks])

            for k0 in range(0, N_SEL, 2 * SUBLANES):
                pair = [load_v(k0), load_v(k0 + SUBLANES)]
                issues[tok](2 * (N_SEL // SUBLANES - PEER_ISSUE_U))
                for wks, vks in pair:
                    for r, (wk, vk) in enumerate(zip(wks, vks)):
                        accs[r % 4] = accs[r % 4] + wk * vk
            issues[tok](N_SEL)
            ltok = i * tb + tok
            o_buf[ltok * SUBLANES:(ltok + 1) * SUBLANES, :] = (accs[0] + accs[1]) + (accs[2] + accs[3])
        if do_issue:
            refill()

    for i in range(nt):
        pl.when(jnp.logical_and(issue_on, jnp.logical_not(comp_on)))(functools.partial(tile_pass, i, True, False))
        pl.when(jnp.logical_and(issue_on, comp_on))(functools.partial(tile_pass, i, True, True))
        pl.when(jnp.logical_and(jnp.logical_not(issue_on), comp_on))(functools.partial(tile_pass, i, False, True))

    @pl.when(comp_on)
    def _():
        out = jnp.concatenate([o_buf[pl.ds(s, step_tok, stride=SUBLANES), :] for s in range(SUBLANES)],
                              axis=-1)
        y_ref[...] = _layer_norm(ALPHA * h1_ref[...] + out, ln2_g_ref[...], ln2_b_ref[...])


def _peer_apply(h1, eidx_t, gate_t, uv_tab, ln2_g, ln2_b):
    t = h1.shape[0]
    tb, nt = PEER_TB, PEER_NT
    step_tok = tb * nt
    assert t % step_tok == 0 and LANES % step_tok == 0 and PEER_SLOTS > nt
    n_tiles = t // tb
    idx_tiles = eidx_t.T.reshape(n_tiles, tb, N_SEL)
    done = lambda j: jnp.maximum(j - 1, 0)
    const = lambda j: (0, 0)
    kern = functools.partial(_peer_apply_kernel, n_tiles=n_tiles)
    return pl.pallas_call(
        kern,
        grid=(n_tiles // nt + 1,),
        in_specs=[
            pl.BlockSpec(memory_space=pl.ANY),
            pl.BlockSpec((step_tok, D_MODEL), lambda j: (done(j), 0)),
            pl.BlockSpec((N_SEL, LANES), lambda j: (0, done(j) * step_tok // LANES)),
            pl.BlockSpec((1, D_MODEL), const),
            pl.BlockSpec((1, D_MODEL), const),
            pl.BlockSpec(memory_space=pl.ANY),
        ],
        out_specs=pl.BlockSpec((step_tok, D_MODEL), lambda j: (done(j), 0)),
        out_shape=jax.ShapeDtypeStruct((t, D_MODEL), _F32),
        scratch_shapes=[
            pltpu.VMEM((PEER_SLOTS, tb * N_SEL, 2) + ROW_TILE, _F32),
            pltpu.SMEM((nt, tb, N_SEL), jnp.int32),
            pltpu.SemaphoreType.DMA((PEER_SLOTS,)),
            pltpu.SemaphoreType.DMA((nt,)),
            pltpu.VMEM((SUBLANES * step_tok, LANES), _F32),
            pltpu.VMEM((N_SEL, tb * LANES), _F32),
            pltpu.VMEM((step_tok * SUBLANES, LANES), _F32),
        ],
        compiler_params=pltpu.CompilerParams(
            dimension_semantics=("arbitrary",),
            vmem_limit_bytes=int(PEER_SLOTS * tb * N_SEL * 2 * 4096 + 12 * 2 ** 20)),
        name="peer_apply",
    )(idx_tiles, h1, gate_t, ln2_g, ln2_b, uv_tab)


def _rope_tables(pos):
    half = RET_DK // 2
    inv = ROPE_BASE ** (-jnp.arange(half, dtype=_F32) / half)
    ang = jnp.asarray(pos, _F32)[:, None] * inv[None, :]
    cos = jnp.cos(ang)
    sin = jnp.sin(ang)
    return jnp.concatenate([cos, cos], axis=-1), jnp.concatenate([-sin, sin], axis=-1)


def _subkey_blocks(sub_keys):
    z = jnp.zeros((PEER_HEADS, N_KEYS, PEER_DK_HALF), sub_keys.dtype)
    top = jnp.concatenate([sub_keys[:, 0], z], axis=-1)
    bot = jnp.concatenate([z, sub_keys[:, 1]], axis=-1)
    return jnp.concatenate([top, bot], axis=1).astype(_BF16)


def kernel(x_prompt, x_sample, cache_meta_k, cache_meta_v, cache_swa_k, cache_swa_v, state_ret,
           meta_tokens, ln_in_g, ln_in_b, rel_bias, w_in, w_out, attn_sinks, ret_gn_g, ln1_g, ln1_b,
           peer_wq, peer_subkeys, peer_u, peer_v, ln2_g, ln2_b):
    b, s, _ = x_prompt.shape
    db, ds, _ = x_sample.shape
    row = lambda a: a.reshape(1, -1)

    w_in_bf = w_in[0].astype(_BF16)
    w_out_bf = w_out[0].astype(_BF16)
    wq_bf = peer_wq[0].astype(_BF16)
    sk_blk = _subkey_blocks(peer_subkeys[0])
    uv_tab = jnp.concatenate([peer_u[0], peer_v[0]], axis=1).reshape((N_EXPERTS, 2) + ROW_TILE)
    lnin_g, lnin_b = row(ln_in_g), row(ln_in_b)
    cos_m, sin_m = _rope_tables(np.arange(N_META))
    cos_p, sin_p = _rope_tables(N_META + np.arange(s))
    cos_s, sin_s = _rope_tables(N_META + PAST_LEN + np.arange(ds))
    bias_tab = _rel_bias_tables(rel_bias)
    sinks = attn_sinks[0]

    _, kv_m, rq_m, rk_m, rv_m, _ = _ln_proj(meta_tokens.astype(x_prompt.dtype), N_META, lnin_g, lnin_b,
                                         w_in_bf, cos_m, sin_m, N_META)
    xp2 = x_prompt.reshape(b * s, D_MODEL)
    xs2 = x_sample.reshape(db * ds, D_MODEL)
    q_p, kv_p, rq_p, rk_p, rv_p, rg_p = _ln_proj(xp2, s, lnin_g, lnin_b, w_in_bf, cos_p, sin_p, PROJ_TM)
    q_s, kv_s, rq_s, rk_s, rv_s, rg_s = _ln_proj(xs2, ds, lnin_g, lnin_b, w_in_bf, cos_s, sin_s, ds)

    kv_p3 = kv_p.reshape(b, s, 2 * KV_W)
    o_att_p = _attention_prompt(q_p.reshape(b, s, ATT_W), kv_p3, kv_m, bias_tab, sinks)
    kv_s3 = kv_s.reshape(db, ds, 2 * KV_W)
    o_att_s = _attention_sample(
        q_s.reshape(db, ds, ATT_W), kv_s3,
        cache_swa_k[0].reshape(db, WINDOW, KV_W), cache_swa_v[0].reshape(db, WINDOW, KV_W),
        cache_meta_k[0].reshape(db, N_META, KV_W), cache_meta_v[0].reshape(db, N_META, KV_W),
        bias_tab, sinks)

    zero_state = jnp.zeros((1, RET_HEADS, RET_DK, RET_DK), _F32)
    _, st_meta = _retention(zero_state, rq_m[None], rk_m[None], rv_m[None], N_META)
    st0_p = jnp.broadcast_to(st_meta, (b, RET_HEADS, RET_DK, RET_DK))
    seq3 = lambda a, n, l: a.reshape(n, l, RET_W)
    o_ret_p, st_p = _retention(st0_p, seq3(rq_p, b, s), seq3(rk_p, b, s), seq3(rv_p, b, s), CHUNK)
    o_ret_s, st_s = _retention(state_ret[0].astype(_F32), seq3(rq_s, db, ds), seq3(rk_s, db, ds),
                               seq3(rv_s, db, ds), ds)

    def tail(x2d, o_att, o_ret, rg):
        h1, eidx_t, gate_t = _mix_route(x2d, o_att, o_ret, rg, lnin_g, lnin_b, row(ret_gn_g[0]),
                                        w_out_bf, row(ln1_g[0]), row(ln1_b[0]), wq_bf, sk_blk)
        return _peer_apply(h1, eidx_t, gate_t, uv_tab, row(ln2_g[0]), row(ln2_b[0]))

    y_p = tail(xp2, o_att_p.reshape(b * s, ATT_W), o_ret_p.reshape(b * s, RET_W), rg_p)
    y_s = tail(xs2, o_att_s.reshape(db * ds, ATT_W), o_ret_s.reshape(db * ds, RET_W), rg_s)

    kvh = lambda a, n, l: a.reshape(1, n, l, KV_HEADS, HEAD_DIM)
    meta_k = jnp.broadcast_to(kvh(kv_m[:, :KV_W], 1, N_META), (1, b, N_META, KV_HEADS, HEAD_DIM))
    meta_v = jnp.broadcast_to(kvh(kv_m[:, KV_W:], 1, N_META), (1, b, N_META, KV_HEADS, HEAD_DIM))
    tail_kv = kv_p3[:, s - WINDOW:]
    return (y_p.reshape(b, s, D_MODEL), y_s.reshape(db, ds, D_MODEL),
            meta_k, meta_v,
            kvh(tail_kv[..., :KV_W], b, WINDOW), kvh(tail_kv[..., KV_W:], b, WINDOW),
            st_p[None],
            kvh(kv_s3[..., :KV_W], db, ds), kvh(kv_s3[..., KV_W:], db, ds),
            st_s[None])
```

```python
import functools
import math

import jax
import jax.numpy as jnp
import numpy as np
from jax import lax
from jax.experimental import pallas as pl
from jax.experimental.pallas import tpu as pltpu

D_MODEL = 1024
CHUNK = 64
N_META = 16
PAST_LEN = 2048
ATT_W = 512
HEAD_DIM = 64
N_HEADS = 8
KV_HEADS = 2
GQA_GROUP = N_HEADS // KV_HEADS
KV_W = KV_HEADS * HEAD_DIM
WINDOW = 128
N_BACK = WINDOW // CHUNK
BAND = (N_BACK + 1) * CHUNK
N_KEYS_ATT = BAND + N_META
N_BUCKETS = 32
MAX_DISTANCE = 128
RET_W = 512
RET_HEADS = 4
RET_DK = 128
ROPE_BASE = 10000.0
PROJ_W = 2816
N_KEYS = 128
N_EXPERTS = N_KEYS * N_KEYS
PEER_HEADS = 8
PEER_TOPK = 16
PEER_DK_HALF = 64
N_SEL = PEER_HEADS * PEER_TOPK
ALPHA = 2.0 ** 0.25
LN_EPS = 1e-5
NEG_INF = -1e30

LANES = 128
SUBLANES = 8
ROW_TILE = (SUBLANES, LANES)

PEER_TB = 8
PEER_NT = 2
PEER_SLOTS = 3
PEER_REGION = 4
PEER_ISSUE_U = 5
ROUTE_TM = 256
PROJ_TM = 256

_F32 = jnp.float32
_BF16 = jnp.bfloat16


def _vmem_limit(nbytes):
    return pltpu.CompilerParams(vmem_limit_bytes=int(nbytes))


def _layer_norm(x, g, b):
    mu = jnp.mean(x, axis=-1, keepdims=True)
    xc = x - mu
    var = jnp.mean(xc * xc, axis=-1, keepdims=True)
    return xc * lax.rsqrt(var + LN_EPS) * g + b


def _ln_proj_kernel(x_ref, g_ref, b_ref, w_ref, cos_ref, sin_ref,
                    q_ref, kv_ref, rq_ref, rk_ref, rv_ref, rg_ref):
    h = _layer_norm(x_ref[...], g_ref[...], b_ref[...])
    p = jnp.dot(h.astype(_BF16), w_ref[...], preferred_element_type=_F32)
    cos = cos_ref[...]
    sin = sin_ref[...]

    def rotary(a):
        outs = []
        for hh in range(RET_HEADS):
            seg = a[:, hh * RET_DK:(hh + 1) * RET_DK]
            outs.append(seg * cos + pltpu.roll(seg, RET_DK // 2, axis=1) * sin)
        return jnp.concatenate(outs, axis=-1)

    q_ref[...] = p[:, 0:512]
    kv_ref[...] = p[:, 512:768]
    rq_ref[...] = rotary(p[:, 768:1280])
    rk_ref[...] = rotary(p[:, 1280:1792]) * (RET_DK ** -0.5)
    rv_ref[...] = p[:, 1792:2304]
    rg_ref[...] = p[:, 2304:2816]


def _ln_proj(x2d, seq_len, ln_g, ln_b, w_in_bf, cos, sin, tm):
    t = x2d.shape[0]
    nblk_s = seq_len // tm
    row = lambda i: (i, 0)
    const = lambda i: (0, 0)
    outs = [jax.ShapeDtypeStruct((t, w), _F32) for w in (512, 256, 512, 512, 512, 512)]
    return pl.pallas_call(
        _ln_proj_kernel,
        grid=(t // tm,),
        in_specs=[
            pl.BlockSpec((tm, D_MODEL), row),
            pl.BlockSpec((1, D_MODEL), const),
            pl.BlockSpec((1, D_MODEL), const),
            pl.BlockSpec((D_MODEL, PROJ_W), const),
            pl.BlockSpec((tm, RET_DK), lambda i: (i % nblk_s, 0)),
            pl.BlockSpec((tm, RET_DK), lambda i: (i % nblk_s, 0)),
        ],
        out_specs=[pl.BlockSpec((tm, w), row) for w in (512, 256, 512, 512, 512, 512)],
        out_shape=outs,
        compiler_params=_vmem_limit(48 * 2 ** 20),
        name="ln_proj",
    )(x2d, ln_g, ln_b, w_in_bf, cos, sin)


def _t5_bucket_np(rel):
    nb = N_BUCKETS // 2
    max_exact = nb // 2
    n = np.abs(rel)
    large = max_exact + (np.log(np.maximum(n, max_exact).astype(np.float32) / max_exact)
                         / math.log(MAX_DISTANCE / max_exact) * (nb - max_exact)).astype(np.int32)
    large = np.minimum(large, nb - 1)
    return (np.where(rel > 0, nb, 0) + np.where(n < max_exact, n, large)).astype(np.int32)


def _bias_buckets():
    i = np.arange(CHUNK)
    jb = np.arange(BAND)
    m = np.arange(N_META)
    rel_band = jb[None, :] - N_BACK * CHUNK - i[:, None]
    out = []
    for c in range(3):
        rel_meta = m[None, :] - (N_META + c * CHUNK + i[:, None])
        out.append(np.concatenate([rel_band, rel_meta], axis=-1))
    return _t5_bucket_np(np.stack(out))


def _rel_bias_kernel(rb_ref, bucket_ref, o_ref):
    h = pl.program_id(1)
    bucket = bucket_ref[0]
    acc = jnp.zeros(bucket.shape, _F32)
    for v in range(N_BUCKETS):
        acc = jnp.where(bucket == v, rb_ref[v, h], acc)
    o_ref[0, 0] = acc


def _rel_bias_tables(rel_bias):
    buckets = jnp.asarray(_bias_buckets())
    return pl.pallas_call(
        _rel_bias_kernel,
        grid=(3, N_HEADS),
        in_specs=[
            pl.BlockSpec(memory_space=pltpu.SMEM),
            pl.BlockSpec((1, CHUNK, N_KEYS_ATT), lambda v, h: (v, 0, 0)),
        ],
        out_specs=pl.BlockSpec((1, 1, CHUNK, N_KEYS_ATT), lambda v, h: (v, h, 0, 0)),
        out_shape=jax.ShapeDtypeStruct((3, N_HEADS, CHUNK, N_KEYS_ATT), _F32),
        name="rel_bias",
    )(rel_bias, buckets)


def _attend(q, kk_band, vv_band, kk_meta, vv_meta, bias_ref, sinks_ref, min_valid_col):
    lane = lax.broadcasted_iota(jnp.int32, (1, LANES), 1)
    lo = lane < HEAD_DIM
    kk = jnp.concatenate([kk_band, kk_meta], axis=0)
    vv = jnp.concatenate([vv_band, vv_meta], axis=0)
    kk_r = pltpu.roll(kk, HEAD_DIM, axis=1)
    vv_r = pltpu.roll(vv, HEAD_DIM, axis=1)
    k_dup = [jnp.where(lo, kk, kk_r).astype(_BF16), jnp.where(lo, kk_r, kk).astype(_BF16)]
    v_dup = [jnp.where(lo, vv, vv_r).astype(_BF16), jnp.where(lo, vv_r, vv).astype(_BF16)]
    rows4 = GQA_GROUP * CHUNK
    col = lax.broadcasted_iota(jnp.int32, (rows4, N_KEYS_ATT), 1)
    valid = col >= min_valid_col
    row = lax.broadcasted_iota(jnp.int32, (rows4, 1), 0)
    outs = []
    for kvh in range(KV_HEADS):
        qa = q[:, (2 * kvh) * LANES:(2 * kvh + 1) * LANES]
        qb = q[:, (2 * kvh + 1) * LANES:(2 * kvh + 2) * LANES]
        lhs = jnp.concatenate([jnp.where(lo, qa, 0.0), jnp.where(lo, 0.0, qa),
                               jnp.where(lo, qb, 0.0), jnp.where(lo, 0.0, qb)], axis=0).astype(_BF16)
        heads = [GQA_GROUP * kvh + i for i in range(GQA_GROUP)]
        s = lax.dot_general(lhs, k_dup[kvh], (((1,), (1,)), ((), ())), preferred_element_type=_F32)
        s = s * (HEAD_DIM ** -0.5) + jnp.concatenate([bias_ref[h] for h in heads], axis=0)
        s = jnp.where(valid, s, NEG_INF)
        sink = jnp.where(row < CHUNK, sinks_ref[heads[0]],
                         jnp.where(row < 2 * CHUNK, sinks_ref[heads[1]],
                                   jnp.where(row < 3 * CHUNK, sinks_ref[heads[2]], sinks_ref[heads[3]])))
        m = jnp.maximum(jnp.max(s, axis=-1, keepdims=True), sink)
        p = jnp.exp(s - m)
        denom = jnp.sum(p, axis=-1, keepdims=True) + jnp.exp(sink - m)
        pn = (p / denom).astype(_BF16)
        o4 = jnp.dot(pn, v_dup[kvh], preferred_element_type=_F32)
        outs.append(jnp.where(lo, o4[0:CHUNK], o4[CHUNK:2 * CHUNK]))
        outs.append(jnp.where(lo, o4[2 * CHUNK:3 * CHUNK], o4[3 * CHUNK:]))
    return jnp.concatenate(outs, axis=-1)


def _attn_prompt_kernel(sinks_ref, q_ref, kv2_ref, kv1_ref, kv0_ref, meta_ref, bias_ref, o_ref):
    c = pl.program_id(1)
    kvs = [kv2_ref[0], kv1_ref[0], kv0_ref[0]]
    kk_band = jnp.concatenate([kv[:, :KV_W] for kv in kvs], axis=0)
    vv_band = jnp.concatenate([kv[:, KV_W:] for kv in kvs], axis=0)
    meta = meta_ref[...]
    min_valid = (N_BACK - jnp.minimum(c, N_BACK)) * CHUNK
    o_ref[0] = _attend(q_ref[0], kk_band, vv_band, meta[:, :KV_W], meta[:, KV_W:],
                       bias_ref.at[0], sinks_ref, min_valid)


def _attention_prompt(q, kv, kv_meta, bias_tab, sinks):
    b, s, _ = q.shape
    nc = s // CHUNK

    def back(j):
        return lambda bi, c: (bi, jnp.maximum(c - j, 0), 0)

    return pl.pallas_call(
        _attn_prompt_kernel,
        grid=(b, nc),
        in_specs=[
            pl.BlockSpec(memory_space=pltpu.SMEM),
            pl.BlockSpec((1, CHUNK, ATT_W), lambda bi, c: (bi, c, 0)),
            pl.BlockSpec((1, CHUNK, 2 * KV_W), back(2)),
            pl.BlockSpec((1, CHUNK, 2 * KV_W), back(1)),
            pl.BlockSpec((1, CHUNK, 2 * KV_W), back(0)),
            pl.BlockSpec((N_META, 2 * KV_W), lambda bi, c: (0, 0)),
            pl.BlockSpec((1, N_HEADS, CHUNK, N_KEYS_ATT),
                         lambda bi, c: (jnp.minimum(c, N_BACK), 0, 0, 0)),
        ],
        out_specs=pl.BlockSpec((1, CHUNK, ATT_W), lambda bi, c: (bi, c, 0)),
        out_shape=jax.ShapeDtypeStruct((b, s, ATT_W), _F32),
        name="attn_prompt",
    )(sinks, q, kv, kv, kv, kv_meta, bias_tab)


def _attn_sample_kernel(sinks_ref, q_ref, kv_ref, ck_ref, cv_ref, mk_ref, mv_ref, bias_ref, o_ref):
    kv = kv_ref[0]
    kk_band = jnp.concatenate([ck_ref[0], kv[:, :KV_W]], axis=0)
    vv_band = jnp.concatenate([cv_ref[0], kv[:, KV_W:]], axis=0)
    o_ref[0] = _attend(q_ref[0], kk_band, vv_band, mk_ref[0], mv_ref[0],
                       bias_ref.at[0], sinks_ref, 0)


def _attention_sample(q, kv, cache_k, cache_v, meta_k, meta_v, bias_tab, sinks):
    b = q.shape[0]
    per_b = lambda bi: (bi, 0, 0)
    return pl.pallas_call(
        _attn_sample_kernel,
        grid=(b,),
        in_specs=[
            pl.BlockSpec(memory_space=pltpu.SMEM),
            pl.BlockSpec((1, CHUNK, ATT_W), per_b),
            pl.BlockSpec((1, CHUNK, 2 * KV_W), per_b),
            pl.BlockSpec((1, WINDOW, KV_W), per_b),
            pl.BlockSpec((1, WINDOW, KV_W), per_b),
            pl.BlockSpec((1, N_META, KV_W), per_b),
            pl.BlockSpec((1, N_META, KV_W), per_b),
            pl.BlockSpec((1, N_HEADS, CHUNK, N_KEYS_ATT), lambda bi: (N_BACK, 0, 0, 0)),
        ],
        out_specs=pl.BlockSpec((1, CHUNK, ATT_W), per_b),
        out_shape=jax.ShapeDtypeStruct((b, CHUNK, ATT_W), _F32),
        name="attn_sample",
    )(sinks, q, kv, cache_k, cache_v, meta_k, meta_v, bias_tab)


def _retention_kernel(gl_ref, st0_ref, rq_ref, rk_ref, rv_ref, decay_ref, qdec_ref, kdec_ref,
                      o_ref, st_ref):
    @pl.when(pl.program_id(0) == 0)
    def _():
        st_ref[...] = st0_ref[...]

    nb = rq_ref.shape[0]
    nt_dims = (((1,), (1,)), ((), ()))
    tn_dims = (((0,), (0,)), ((), ()))
    for b in range(nb):
        outs = []
        for h in range(RET_HEADS):
            sl = slice(h * RET_DK, (h + 1) * RET_DK)
            q = rq_ref[b, :, sl]
            k = rk_ref[b, :, sl]
            v = rv_ref[b, :, sl]
            qb = q.astype(_BF16)
            vb = v.astype(_BF16)
            s = lax.dot_general(qb, k.astype(_BF16), nt_dims, preferred_element_type=_F32)
            s = s * decay_ref[h]
            inner = jnp.dot(s.astype(_BF16), vb, preferred_element_type=_F32)
            st = st_ref[b, h]
            cross = jnp.dot(qb, st.astype(_BF16), preferred_element_type=_F32) * qdec_ref[h]
            outs.append(inner + cross)
            kd = (k * kdec_ref[h]).astype(_BF16)
            upd = lax.dot_general(kd, vb, tn_dims, preferred_element_type=_F32)
            st_ref[b, h] = gl_ref[h] * st + upd
        o_ref[b] = jnp.concatenate(outs, axis=-1)


def _retention_tables(chunk_len):
    log_gamma = jnp.log(1.0 - 2.0 ** (-5.0 - jnp.arange(RET_HEADS, dtype=_F32)))
    idx = jnp.arange(chunk_len, dtype=_F32)
    diff = idx[:, None] - idx[None, :]
    decay = jnp.where(diff >= 0, jnp.exp(jnp.maximum(diff, 0.0)[None] * log_gamma[:, None, None]), 0.0)
    q_dec = jnp.exp((idx + 1.0)[None, :] * log_gamma[:, None])
    k_dec = jnp.exp((chunk_len - 1.0 - idx)[None, :] * log_gamma[:, None])
    bcast = lambda t: jnp.broadcast_to(t[:, :, None], (RET_HEADS, chunk_len, RET_DK))
    g_len = jnp.exp(chunk_len * log_gamma)
    return g_len, decay, bcast(q_dec), bcast(k_dec)


def _retention(state0, rq, rk, rv, chunk_len):
    b, s, _ = rq.shape
    g_len, decay, q_dec, k_dec = _retention_tables(chunk_len)
    seq = pl.BlockSpec((b, chunk_len, RET_W), lambda c: (0, c, 0))
    whole4 = pl.BlockSpec((b, RET_HEADS, RET_DK, RET_DK), lambda c: (0, 0, 0, 0))
    tab = lambda n: pl.BlockSpec((RET_HEADS, chunk_len, n), lambda c: (0, 0, 0))
    return pl.pallas_call(
        _retention_kernel,
        grid=(s // chunk_len,),
        in_specs=[pl.BlockSpec(memory_space=pltpu.SMEM), whole4, seq, seq, seq,
                  tab(chunk_len), tab(RET_DK), tab(RET_DK)],
        out_specs=[seq, whole4],
        out_shape=[jax.ShapeDtypeStruct((b, s, RET_W), _F32),
                   jax.ShapeDtypeStruct((b, RET_HEADS, RET_DK, RET_DK), _F32)],
        compiler_params=pltpu.CompilerParams(dimension_semantics=("arbitrary",)),
        name="retention",
    )(g_len, state0, rq, rk, rv, decay, q_dec, k_dec)


def _top16_rows(x, ids=None, sentinel=None):
    if ids is None:
        ids = lax.broadcasted_iota(jnp.int32, x.shape, 0)
        sentinel = x.shape[0]
    vals, idxs = [], []
    for _ in range(PEER_TOPK):
        m = jnp.max(x, axis=0, keepdims=True)
        idx = jnp.min(jnp.where(x == m, ids, sentinel), axis=0, keepdims=True)
        vals.append(m)
        idxs.append(idx)
        x = jnp.where(ids == idx, -jnp.inf, x)
    return vals, idxs


def _top16_of_128(x):
    n_slab = N_KEYS // SUBLANES
    cols = x.shape[1]
    sub = lax.broadcasted_iota(jnp.int32, (SUBLANES, cols), 0)
    vals = [x[v * SUBLANES:(v + 1) * SUBLANES] for v in range(n_slab)]
    rows = [sub + v * SUBLANES for v in range(n_slab)]
    for p in range(n_slab):
        for a in range(p % 2, n_slab - 1, 2):
            lt = vals[a] < vals[a + 1]
            vals[a], vals[a + 1] = jnp.where(lt, vals[a + 1], vals[a]), jnp.where(lt, vals[a], vals[a + 1])
            rows[a], rows[a + 1] = jnp.where(lt, rows[a + 1], rows[a]), jnp.where(lt, rows[a], rows[a + 1])
    out_v, out_i = [], []
    for r in range(PEER_TOPK):
        m = jnp.max(vals[0], axis=0, keepdims=True)
        idx = jnp.min(jnp.where(vals[0] == m, rows[0], N_KEYS), axis=0, keepdims=True)
        out_v.append(m)
        out_i.append(idx)
        pop = rows[0] == idx
        for q in range(PEER_TOPK - 1 - r):
            vals[q] = jnp.where(pop, vals[q + 1], vals[q])
            rows[q] = jnp.where(pop, rows[q + 1], rows[q])
    return out_v, out_i


def _candidate_pairs():
    return [(a, b) for a in range(PEER_TOPK) for b in range(PEER_TOPK) if (a + 1) * (b + 1) <= PEER_TOPK]


N_CAND = 56


def _mix_route_kernel(x_ref, oatt_ref, oret_ref, rg_ref, lnin_g_ref, lnin_b_ref, gn_ref, wout_ref,
                      ln1_g_ref, ln1_b_ref, wq_ref, sk_ref, cid_ref,
                      h1_ref, eidx_ref, gate_ref, qs_ref):
    h = _layer_norm(x_ref[...], lnin_g_ref[...], lnin_b_ref[...])
    oret = oret_ref[...]
    rg = rg_ref[...]
    gn = gn_ref[...]
    ys = []
    for hh in range(RET_HEADS):
        sl = slice(hh * RET_DK, (hh + 1) * RET_DK)
        seg = oret[:, sl]
        mu = jnp.mean(seg, axis=-1, keepdims=True)
        sc = seg - mu
        var = jnp.mean(sc * sc, axis=-1, keepdims=True)
        gate = rg[:, sl]
        ys.append(sc * lax.rsqrt(var + LN_EPS) * gn[:, sl] * (gate * jax.nn.sigmoid(gate)))
    mixed = jnp.concatenate([oatt_ref[...]] + ys, axis=-1).astype(_BF16)
    mix = jnp.dot(mixed, wout_ref[...], preferred_element_type=_F32)
    h1 = _layer_norm(ALPHA * h + mix, ln1_g_ref[...], ln1_b_ref[...])
    h1_ref[...] = h1

    qp = jnp.dot(h1.astype(_BF16), wq_ref[...], preferred_element_type=_F32).astype(_BF16)
    for p in range(PEER_HEADS):
        qs_ref[p] = qp[:, p * LANES:(p + 1) * LANES]

    tm = x_ref.shape[0]
    row16 = lax.broadcasted_iota(jnp.int32, (PEER_TOPK, tm), 0)

    def head_body(p, carry):
        sc_t = lax.dot_general(sk_ref[p], qs_ref[p], (((1,), (1,)), ((), ())),
                               preferred_element_type=_F32)
        s1, i1 = _top16_of_128(sc_t[:N_KEYS])
        s2, i2 = _top16_of_128(sc_t[N_KEYS:])
        i1c = jnp.concatenate(i1, axis=0)
        i2c = jnp.concatenate(i2, axis=0)
        pairs = _candidate_pairs()
        pad = [jnp.full_like(s1[0], -jnp.inf)] * (N_CAND - len(pairs))
        cand = jnp.concatenate([s1[a] + s2[b] for a, b in pairs] + pad, axis=0)
        top, sel = _top16_rows(cand, cid_ref[...], PEER_TOPK * PEER_TOPK)
        e_rows = []
        for r in range(PEER_TOPK):
            a = sel[r] >> 4
            b = sel[r] & (PEER_TOPK - 1)
            e1 = jnp.sum(jnp.where(row16 == a, i1c, 0), axis=0, keepdims=True)
            e2 = jnp.sum(jnp.where(row16 == b, i2c, 0), axis=0, keepdims=True)
            e_rows.append(e1 * N_KEYS + e2)
        topc = jnp.concatenate(top, axis=0)
        ex = jnp.exp(topc - top[0])
        gate = ex / jnp.sum(ex, axis=0, keepdims=True)
        off = pl.multiple_of(p * PEER_TOPK, PEER_TOPK)
        eidx_ref[pl.ds(off, PEER_TOPK), :] = jnp.concatenate(e_rows, axis=0)
        gate_ref[pl.ds(off, PEER_TOPK), :] = gate
        return carry

    lax.fori_loop(0, PEER_HEADS, head_body, 0)


def _mix_route(x2d, o_att, o_ret, rg, lnin_g, lnin_b, gn_g, w_out_bf, ln1_g, ln1_b, wq_bf, sk_blk):
    t = x2d.shape[0]
    tm = ROUTE_TM
    row = lambda i: (i, 0)
    const = lambda i: (0, 0)
    vec = pl.BlockSpec((1, D_MODEL), const)
    assert t % tm == 0
    flat = [a * PEER_TOPK + b for a, b in _candidate_pairs()]
    flat += [PEER_TOPK * PEER_TOPK] * (N_CAND - len(flat))
    cand_ids = jnp.asarray(np.broadcast_to(np.asarray(flat, np.int32)[:, None], (N_CAND, tm)))
    return pl.pallas_call(
        _mix_route_kernel,
        grid=(t // tm,),
        in_specs=[
            pl.BlockSpec((tm, D_MODEL), row),
            pl.BlockSpec((tm, ATT_W), row),
            pl.BlockSpec((tm, RET_W), row),
            pl.BlockSpec((tm, RET_W), row),
            vec, vec,
            pl.BlockSpec((1, RET_W), const),
            pl.BlockSpec((D_MODEL, D_MODEL), const),
            vec, vec,
            pl.BlockSpec((D_MODEL, D_MODEL), const),
            pl.BlockSpec((PEER_HEADS, 2 * N_KEYS, LANES), lambda i: (0, 0, 0)),
            pl.BlockSpec((N_CAND, tm), const),
        ],
        out_specs=[
            pl.BlockSpec((tm, D_MODEL), row),
            pl.BlockSpec((N_SEL, tm), lambda i: (0, i)),
            pl.BlockSpec((N_SEL, tm), lambda i: (0, i)),
        ],
        out_shape=[
            jax.ShapeDtypeStruct((t, D_MODEL), _F32),
            jax.ShapeDtypeStruct((N_SEL, t), jnp.int32),
            jax.ShapeDtypeStruct((N_SEL, t), _F32),
        ],
        scratch_shapes=[pltpu.VMEM((PEER_HEADS, tm, LANES), _BF16)],
        compiler_params=_vmem_limit(40 * 2 ** 20),
        name="mix_route",
    )(x2d, o_att, o_ret, rg, lnin_g, lnin_b, gn_g, w_out_bf, ln1_g, ln1_b, wq_bf, sk_blk, cand_ids)


PACK_ROWS = 512


def _pack_uv_kernel(u_ref, v_ref, o_ref, stage_ref):
    def group(g, carry):
        r0 = pl.multiple_of(g * SUBLANES, SUBLANES)
        for t, src in enumerate((u_ref, v_ref)):
            blk = src[pl.ds(r0, SUBLANES), :]
            for s in range(SUBLANES):
                stage_ref[t, s * SUBLANES:(s + 1) * SUBLANES, :] = blk[:, s * LANES:(s + 1) * LANES]
            for r in range(SUBLANES):
                o_ref[r0 + r, t] = stage_ref[t, pl.ds(r, SUBLANES, stride=SUBLANES), :]
        return carry

    lax.fori_loop(0, u_ref.shape[0] // SUBLANES, group, 0)


def _pack_uv(u_tab, v_tab):
    n = u_tab.shape[0]
    assert n % PACK_ROWS == 0
    row = lambda i: (i, 0)
    return pl.pallas_call(
        _pack_uv_kernel,
        grid=(n // PACK_ROWS,),
        in_specs=[pl.BlockSpec((PACK_ROWS, D_MODEL), row), pl.BlockSpec((PACK_ROWS, D_MODEL), row)],
        out_specs=pl.BlockSpec((PACK_ROWS, 2) + ROW_TILE, lambda i: (i, 0, 0, 0)),
        out_shape=jax.ShapeDtypeStruct((n, 2) + ROW_TILE, _F32),
        scratch_shapes=[pltpu.VMEM((2, SUBLANES * SUBLANES, LANES), _F32)],
        name="pack_uv",
    )(u_tab, v_tab)


_BITREV8 = (0, 4, 2, 6, 1, 5, 3, 7)


def _rows_to_sublanes(tiles, sub):
    m4 = sub < 4
    m2 = (sub & 2) == 0
    m1 = (sub & 1) == 0
    t = [tiles[i] for i in _BITREV8]
    c = []
    for a, b in ((0, 1), (2, 3), (4, 5), (6, 7)):
        w = jnp.where(m4, t[a], t[b])
        x = jnp.where(m4, t[b], t[a])
        c.append(w + pltpu.roll(x, 4, axis=0))
    d = []
    for a, b in ((0, 1), (2, 3)):
        d.append(jnp.where(m2, c[a] + pltpu.roll(c[a], 6, axis=0), c[b] + pltpu.roll(c[b], 2, axis=0)))
    return jnp.where(m1, d[0] + pltpu.roll(d[0], 7, axis=0), d[1] + pltpu.roll(d[1], 1, axis=0))


def _peer_apply_kernel(idx_hbm, h1_ref, gate_ref, ln2_g_ref, ln2_b_ref, uv_hbm,
                       y_ref, uvbuf, idx_smem, gsem, isem, xs_ref, wb_ref, o_buf, *, n_tiles):
    tb, nt = PEER_TB, PEER_NT
    step_tok = tb * nt
    n_steps = n_tiles // nt
    j = pl.program_id(0)
    issue_on = j < n_steps
    comp_on = j >= 1
    lane0 = ((j - 1) * step_tok) % LANES

    def idx_copy(tile, buf):
        return pltpu.make_async_copy(idx_hbm.at[tile], idx_smem.at[buf], isem.at[buf])

    @pl.when(j == 0)
    def _():
        for i in range(nt):
            idx_copy(i, i).start()

    @pl.when(comp_on)
    def _():
        h1 = h1_ref[...]
        for s in range(SUBLANES):
            xs_ref[s * step_tok:(s + 1) * step_tok, :] = h1[:, s * LANES:(s + 1) * LANES]

    def tile_pass(i, do_issue, do_comp):
        n_groups = N_SEL // SUBLANES
        tile_new = nt * j + i
        islot = tile_new % PEER_SLOTS
        cslot = (tile_new - nt) % PEER_SLOTS
        if do_issue:
            idx_copy(tile_new, i).wait()
        if do_comp:
            pltpu.make_async_copy(uv_hbm.at[pl.ds(0, tb * N_SEL)], uvbuf.at[cslot], gsem.at[cslot]).wait()

        def make_issue(tok):
            pending = list(range(N_SEL)) if do_issue else []

            def issue(n):
                for _ in range(min(n, len(pending))):
                    k = pending.pop(0)
                    e = idx_smem[i, tok, k]
                    pltpu.make_async_copy(uv_hbm.at[e], uvbuf.at[islot, tok * N_SEL + k],
                                          gsem.at[islot]).start(priority=k % 2)
            return issue

        def refill():
            @pl.when(tile_new + nt < n_tiles)
            def _():
                idx_copy(tile_new + nt, i).start()

        if not do_comp:
            def issue_body(tok, carry):
                make_issue(tok)(N_SEL)
                return carry
            lax.fori_loop(0, tb, issue_body, 0)
            refill()
            return

        issues = [make_issue(tok) for tok in range(tb)]
        sub = lax.broadcasted_iota(jnp.int32, ROW_TILE, 0)

        lane = lax.broadcasted_iota(jnp.int32, (N_SEL, LANES), 1)
        act = jnp.zeros((N_SEL, LANES), _F32)
        for tok in range(tb):
            xt = xs_ref[pl.ds(i * tb + tok, SUBLANES, stride=step_tok), :]
            parts = []

            def load_u(g):
                return [uvbuf[cslot, tok * N_SEL + g * SUBLANES + r, 0] for r in range(SUBLANES)]

            for g0 in range(0, n_groups, PEER_REGION):
                region = [load_u(g) for g in range(g0, g0 + PEER_REGION)]
                issues[tok](PEER_ISSUE_U * PEER_REGION)
                for rows in region:
                    prods = [row * xt for row in rows]
                    parts.append(jnp.sum(_rows_to_sublanes(prods, sub), axis=1, keepdims=True))
            act = jnp.where(lane == tok, jnp.concatenate(parts, axis=0), act)

        gate0 = pltpu.roll(gate_ref[...], (2 * LANES - lane0 - i * tb) % LANES, axis=1)
        w = jax.nn.gelu(act) * gate0
        for tok in range(tb):
            wb_ref[:, tok * LANES:(tok + 1) * LANES] = jnp.broadcast_to(w[:, tok:tok + 1], (N_SEL, LANES))

        for tok in range(tb):
            accs = [jnp.zeros(ROW_TILE, _F32) for _ in range(4)]

            def load_v(k0):
                ks = range(k0, k0 + SUBLANES)
                return ([jnp.broadcast_to(wb_ref[k:k + 1, tok * LANES:(tok + 1) * LANES], ROW_TILE) for k in ks],
                        [uvbuf[cslot, tok * N_SEL + k, 1] for k in ks])

            for k0 in range(0, N_SEL, 2 * SUBLANES):
                pair = [load_v(k0), load_v(k0 + SUBLANES)]
                issues[tok](2 * (N_SEL // SUBLANES - PEER_ISSUE_U))
                for wks, vks in pair:
                    for r, (wk, vk) in enumerate(zip(wks, vks)):
                        accs[r % 4] = accs[r % 4] + wk * vk
            issues[tok](N_SEL)
            ltok = i * tb + tok
            o_buf[ltok * SUBLANES:(ltok + 1) * SUBLANES, :] = (accs[0] + accs[1]) + (accs[2] + accs[3])
        if do_issue:
            refill()

    for i in range(nt):
        pl.when(jnp.logical_and(issue_on, jnp.logical_not(comp_on)))(functools.partial(tile_pass, i, True, False))
        pl.when(jnp.logical_and(issue_on, comp_on))(functools.partial(tile_pass, i, True, True))
        pl.when(jnp.logical_and(jnp.logical_not(issue_on), comp_on))(functools.partial(tile_pass, i, False, True))

    @pl.when(comp_on)
    def _():
        out = jnp.concatenate([o_buf[pl.ds(s, step_tok, stride=SUBLANES), :] for s in range(SUBLANES)],
                              axis=-1)
        y_ref[...] = _layer_norm(ALPHA * h1_ref[...] + out, ln2_g_ref[...], ln2_b_ref[...])


def _peer_apply(h1, eidx_t, gate_t, uv_tab, ln2_g, ln2_b):
    t = h1.shape[0]
    tb, nt = PEER_TB, PEER_NT
    step_tok = tb * nt
    assert t % step_tok == 0 and LANES % step_tok == 0 and PEER_SLOTS > nt
    n_tiles = t // tb
    idx_tiles = eidx_t.T.reshape(n_tiles, tb, N_SEL)
    done = lambda j: jnp.maximum(j - 1, 0)
    const = lambda j: (0, 0)
    kern = functools.partial(_peer_apply_kernel, n_tiles=n_tiles)
    return pl.pallas_call(
        kern,
        grid=(n_tiles // nt + 1,),
        in_specs=[
            pl.BlockSpec(memory_space=pl.ANY),
            pl.BlockSpec((step_tok, D_MODEL), lambda j: (done(j), 0)),
            pl.BlockSpec((N_SEL, LANES), lambda j: (0, done(j) * step_tok // LANES)),
            pl.BlockSpec((1, D_MODEL), const),
            pl.BlockSpec((1, D_MODEL), const),
            pl.BlockSpec(memory_space=pl.ANY),
        ],
        out_specs=pl.BlockSpec((step_tok, D_MODEL), lambda j: (done(j), 0)),
        out_shape=jax.ShapeDtypeStruct((t, D_MODEL), _F32),
        scratch_shapes=[
            pltpu.VMEM((PEER_SLOTS, tb * N_SEL, 2) + ROW_TILE, _F32),
            pltpu.SMEM((nt, tb, N_SEL), jnp.int32),
            pltpu.SemaphoreType.DMA((PEER_SLOTS,)),
            pltpu.SemaphoreType.DMA((nt,)),
            pltpu.VMEM((SUBLANES * step_tok, LANES), _F32),
            pltpu.VMEM((N_SEL, tb * LANES), _F32),
            pltpu.VMEM((step_tok * SUBLANES, LANES), _F32),
        ],
        compiler_params=pltpu.CompilerParams(
            dimension_semantics=("arbitrary",),
            vmem_limit_bytes=int(PEER_SLOTS * tb * N_SEL * 2 * 4096 + 12 * 2 ** 20)),
        name="peer_apply",
    )(idx_tiles, h1, gate_t, ln2_g, ln2_b, uv_tab)


def _rope_tables(pos):
    half = RET_DK // 2
    inv = ROPE_BASE ** (-jnp.arange(half, dtype=_F32) / half)
    ang = jnp.asarray(pos, _F32)[:, None] * inv[None, :]
    cos = jnp.cos(ang)
    sin = jnp.sin(ang)
    return jnp.concatenate([cos, cos], axis=-1), jnp.concatenate([-sin, sin], axis=-1)


def _subkey_blocks(sub_keys):
    z = jnp.zeros((PEER_HEADS, N_KEYS, PEER_DK_HALF), sub_keys.dtype)
    top = jnp.concatenate([sub_keys[:, 0], z], axis=-1)
    bot = jnp.concatenate([z, sub_keys[:, 1]], axis=-1)
    return jnp.concatenate([top, bot], axis=1).astype(_BF16)


def kernel(x_prompt, x_sample, cache_meta_k, cache_meta_v, cache_swa_k, cache_swa_v, state_ret,
           meta_tokens, ln_in_g, ln_in_b, rel_bias, w_in, w_out, attn_sinks, ret_gn_g, ln1_g, ln1_b,
           peer_wq, peer_subkeys, peer_u, peer_v, ln2_g, ln2_b):
    b, s, _ = x_prompt.shape
    db, ds, _ = x_sample.shape
    row = lambda a: a.reshape(1, -1)

    w_in_bf = w_in[0].astype(_BF16)
    w_out_bf = w_out[0].astype(_BF16)
    wq_bf = peer_wq[0].astype(_BF16)
    sk_blk = _subkey_blocks(peer_subkeys[0])
    uv_tab = _pack_uv(peer_u[0], peer_v[0])
    lnin_g, lnin_b = row(ln_in_g), row(ln_in_b)
    cos_m, sin_m = _rope_tables(np.arange(N_META))
    cos_p, sin_p = _rope_tables(N_META + np.arange(s))
    cos_s, sin_s = _rope_tables(N_META + PAST_LEN + np.arange(ds))
    bias_tab = _rel_bias_tables(rel_bias)
    sinks = attn_sinks[0]

    _, kv_m, rq_m, rk_m, rv_m, _ = _ln_proj(meta_tokens.astype(x_prompt.dtype), N_META, lnin_g, lnin_b,
                                         w_in_bf, cos_m, sin_m, N_META)
    xp2 = x_prompt.reshape(b * s, D_MODEL)
    xs2 = x_sample.reshape(db * ds, D_MODEL)
    q_p, kv_p, rq_p, rk_p, rv_p, rg_p = _ln_proj(xp2, s, lnin_g, lnin_b, w_in_bf, cos_p, sin_p, PROJ_TM)
    q_s, kv_s, rq_s, rk_s, rv_s, rg_s = _ln_proj(xs2, ds, lnin_g, lnin_b, w_in_bf, cos_s, sin_s, ds)

    kv_p3 = kv_p.reshape(b, s, 2 * KV_W)
    o_att_p = _attention_prompt(q_p.reshape(b, s, ATT_W), kv_p3, kv_m, bias_tab, sinks)
    kv_s3 = kv_s.reshape(db, ds, 2 * KV_W)
    o_att_s = _attention_sample(
        q_s.reshape(db, ds, ATT_W), kv_s3,
        cache_swa_k[0].reshape(db, WINDOW, KV_W), cache_swa_v[0].reshape(db, WINDOW, KV_W),
        cache_meta_k[0].reshape(db, N_META, KV_W), cache_meta_v[0].reshape(db, N_META, KV_W),
        bias_tab, sinks)

    zero_state = jnp.zeros((1, RET_HEADS, RET_DK, RET_DK), _F32)
    _, st_meta = _retention(zero_state, rq_m[None], rk_m[None], rv_m[None], N_META)
    st0_p = jnp.broadcast_to(st_meta, (b, RET_HEADS, RET_DK, RET_DK))
    seq3 = lambda a, n, l: a.reshape(n, l, RET_W)
    o_ret_p, st_p = _retention(st0_p, seq3(rq_p, b, s), seq3(rk_p, b, s), seq3(rv_p, b, s), CHUNK)
    o_ret_s, st_s = _retention(state_ret[0].astype(_F32), seq3(rq_s, db, ds), seq3(rk_s, db, ds),
                               seq3(rv_s, db, ds), ds)

    def tail(x2d, o_att, o_ret, rg):
        h1, eidx_t, gate_t = _mix_route(x2d, o_att, o_ret, rg, lnin_g, lnin_b, row(ret_gn_g[0]),
                                        w_out_bf, row(ln1_g[0]), row(ln1_b[0]), wq_bf, sk_blk)
        return _peer_apply(h1, eidx_t, gate_t, uv_tab, row(ln2_g[0]), row(ln2_b[0]))

    y_p = tail(xp2, o_att_p.reshape(b * s, ATT_W), o_ret_p.reshape(b * s, RET_W), rg_p)
    y_s = tail(xs2, o_att_s.reshape(db * ds, ATT_W), o_ret_s.reshape(db * ds, RET_W), rg_s)

    kvh = lambda a, n, l: a.reshape(1, n, l, KV_HEADS, HEAD_DIM)
    meta_k = jnp.broadcast_to(kvh(kv_m[:, :KV_W], 1, N_META), (1, b, N_META, KV_HEADS, HEAD_DIM))
    meta_v = jnp.broadcast_to(kvh(kv_m[:, KV_W:], 1, N_META), (1, b, N_META, KV_HEADS, HEAD_DIM))
    tail_kv = kv_p3[:, s - WINDOW:]
    return (y_p.reshape(b, s, D_MODEL), y_s.reshape(db, ds, D_MODEL),
            meta_k, meta_v,
            kvh(tail_kv[..., :KV_W], b, WINDOW), kvh(tail_kv[..., KV_W:], b, WINDOW),
            st_p[None],
            kvh(kv_s3[..., :KV_W], db, ds), kvh(kv_s3[..., KV_W:], db, ds),
            st_s[None])
```

```python
import functools
import math

import jax
import jax.numpy as jnp
import numpy as np
from jax import lax
from jax.experimental import pallas as pl
from jax.experimental.pallas import tpu as pltpu

D_MODEL = 1024
CHUNK = 64
N_META = 16
PAST_LEN = 2048
ATT_W = 512
HEAD_DIM = 64
N_HEADS = 8
KV_HEADS = 2
GQA_GROUP = N_HEADS // KV_HEADS
KV_W = KV_HEADS * HEAD_DIM
WINDOW = 128
N_BACK = WINDOW // CHUNK
BAND = (N_BACK + 1) * CHUNK
N_KEYS_ATT = BAND + N_META
N_BUCKETS = 32
MAX_DISTANCE = 128
RET_W = 512
RET_HEADS = 4
RET_DK = 128
ROPE_BASE = 10000.0
PROJ_W = 2816
N_KEYS = 128
N_EXPERTS = N_KEYS * N_KEYS
PEER_HEADS = 8
PEER_TOPK = 16
PEER_DK_HALF = 64
N_SEL = PEER_HEADS * PEER_TOPK
ALPHA = 2.0 ** 0.25
LN_EPS = 1e-5
NEG_INF = -1e30

LANES = 128
SUBLANES = 8
ROW_TILE = (SUBLANES, LANES)

PEER_TB = 8
PEER_NT = 2
PEER_SLOTS = 3
PEER_REGION = 4
PEER_ISSUE_U = 5
ROUTE_TM = 256
PROJ_TM = 256

_F32 = jnp.float32
_BF16 = jnp.bfloat16


def _vmem_limit(nbytes):
    return pltpu.CompilerParams(vmem_limit_bytes=int(nbytes))


def _layer_norm(x, g, b):
    mu = jnp.mean(x, axis=-1, keepdims=True)
    xc = x - mu
    var = jnp.mean(xc * xc, axis=-1, keepdims=True)
    return xc * lax.rsqrt(var + LN_EPS) * g + b


def _ln_proj_kernel(x_ref, g_ref, b_ref, w_ref, cos_ref, sin_ref,
                    q_ref, kv_ref, rq_ref, rk_ref, rv_ref, rg_ref):
    h = _layer_norm(x_ref[...], g_ref[...], b_ref[...])
    p = jnp.dot(h.astype(_BF16), w_ref[...], preferred_element_type=_F32)
    cos = cos_ref[...]
    sin = sin_ref[...]

    def rotary(a):
        outs = []
        for hh in range(RET_HEADS):
            seg = a[:, hh * RET_DK:(hh + 1) * RET_DK]
            outs.append(seg * cos + pltpu.roll(seg, RET_DK // 2, axis=1) * sin)
        return jnp.concatenate(outs, axis=-1)

    q_ref[...] = p[:, 0:512]
    kv_ref[...] = p[:, 512:768]
    rq_ref[...] = rotary(p[:, 768:1280])
    rk_ref[...] = rotary(p[:, 1280:1792]) * (RET_DK ** -0.5)
    rv_ref[...] = p[:, 1792:2304]
    rg_ref[...] = p[:, 2304:2816]


def _ln_proj(x2d, seq_len, ln_g, ln_b, w_in_bf, cos, sin, tm):
    t = x2d.shape[0]
    nblk_s = seq_len // tm
    row = lambda i: (i, 0)
    const = lambda i: (0, 0)
    outs = [jax.ShapeDtypeStruct((t, w), _F32) for w in (512, 256, 512, 512, 512, 512)]
    return pl.pallas_call(
        _ln_proj_kernel,
        grid=(t // tm,),
        in_specs=[
            pl.BlockSpec((tm, D_MODEL), row),
            pl.BlockSpec((1, D_MODEL), const),
            pl.BlockSpec((1, D_MODEL), const),
            pl.BlockSpec((D_MODEL, PROJ_W), const),
            pl.BlockSpec((tm, RET_DK), lambda i: (i % nblk_s, 0)),
            pl.BlockSpec((tm, RET_DK), lambda i: (i % nblk_s, 0)),
        ],
        out_specs=[pl.BlockSpec((tm, w), row) for w in (512, 256, 512, 512, 512, 512)],
        out_shape=outs,
        compiler_params=_vmem_limit(48 * 2 ** 20),
        name="ln_proj",
    )(x2d, ln_g, ln_b, w_in_bf, cos, sin)


def _t5_bucket_np(rel):
    nb = N_BUCKETS // 2
    max_exact = nb // 2
    n = np.abs(rel)
    large = max_exact + (np.log(np.maximum(n, max_exact).astype(np.float32) / max_exact)
                         / math.log(MAX_DISTANCE / max_exact) * (nb - max_exact)).astype(np.int32)
    large = np.minimum(large, nb - 1)
    return (np.where(rel > 0, nb, 0) + np.where(n < max_exact, n, large)).astype(np.int32)


def _bias_buckets():
    i = np.arange(CHUNK)
    jb = np.arange(BAND)
    m = np.arange(N_META)
    rel_band = jb[None, :] - N_BACK * CHUNK - i[:, None]
    out = []
    for c in range(3):
        rel_meta = m[None, :] - (N_META + c * CHUNK + i[:, None])
        out.append(np.concatenate([rel_band, rel_meta], axis=-1))
    return _t5_bucket_np(np.stack(out))


def _rel_bias_kernel(rb_ref, bucket_ref, o_ref):
    h = pl.program_id(1)
    bucket = bucket_ref[0]
    acc = jnp.zeros(bucket.shape, _F32)
    for v in range(N_BUCKETS):
        acc = jnp.where(bucket == v, rb_ref[v, h], acc)
    o_ref[0, 0] = acc


def _rel_bias_tables(rel_bias):
    buckets = jnp.asarray(_bias_buckets())
    return pl.pallas_call(
        _rel_bias_kernel,
        grid=(3, N_HEADS),
        in_specs=[
            pl.BlockSpec(memory_space=pltpu.SMEM),
            pl.BlockSpec((1, CHUNK, N_KEYS_ATT), lambda v, h: (v, 0, 0)),
        ],
        out_specs=pl.BlockSpec((1, 1, CHUNK, N_KEYS_ATT), lambda v, h: (v, h, 0, 0)),
        out_shape=jax.ShapeDtypeStruct((3, N_HEADS, CHUNK, N_KEYS_ATT), _F32),
        name="rel_bias",
    )(rel_bias, buckets)


def _attend(chunks, sinks_ref):
    lane = lax.broadcasted_iota(jnp.int32, (1, LANES), 1)
    lo = lane < HEAD_DIM
    rows4 = GQA_GROUP * CHUNK
    col = lax.broadcasted_iota(jnp.int32, (rows4, N_KEYS_ATT), 1)
    row = lax.broadcasted_iota(jnp.int32, (rows4, 1), 0)
    jobs = []
    for q, kk_band, vv_band, kk_meta, vv_meta, bias_ref, min_valid_col in chunks:
        kk = jnp.concatenate([kk_band, kk_meta], axis=0)
        vv = jnp.concatenate([vv_band, vv_meta], axis=0)
        kk_r = pltpu.roll(kk, HEAD_DIM, axis=1)
        vv_r = pltpu.roll(vv, HEAD_DIM, axis=1)
        k_dup = [jnp.where(lo, kk, kk_r).astype(_BF16), jnp.where(lo, kk_r, kk).astype(_BF16)]
        v_dup = [jnp.where(lo, vv, vv_r).astype(_BF16), jnp.where(lo, vv_r, vv).astype(_BF16)]
        valid = col >= min_valid_col
        for kvh in range(KV_HEADS):
            qa = q[:, (2 * kvh) * LANES:(2 * kvh + 1) * LANES]
            qb = q[:, (2 * kvh + 1) * LANES:(2 * kvh + 2) * LANES]
            lhs = jnp.concatenate([jnp.where(lo, qa, 0.0), jnp.where(lo, 0.0, qa),
                                   jnp.where(lo, qb, 0.0), jnp.where(lo, 0.0, qb)], axis=0).astype(_BF16)
            s = lax.dot_general(lhs, k_dup[kvh], (((1,), (1,)), ((), ())), preferred_element_type=_F32)
            jobs.append((s, kvh, bias_ref, valid, v_dup[kvh]))
    probs = []
    for s, kvh, bias_ref, valid, v_rows in jobs:
        heads = [GQA_GROUP * kvh + i for i in range(GQA_GROUP)]
        s = s * (HEAD_DIM ** -0.5) + jnp.concatenate([bias_ref[h] for h in heads], axis=0)
        s = jnp.where(valid, s, NEG_INF)
        sink = jnp.where(row < CHUNK, sinks_ref[heads[0]],
                         jnp.where(row < 2 * CHUNK, sinks_ref[heads[1]],
                                   jnp.where(row < 3 * CHUNK, sinks_ref[heads[2]], sinks_ref[heads[3]])))
        m = jnp.maximum(jnp.max(s, axis=-1, keepdims=True), sink)
        p = jnp.exp(s - m)
        denom = jnp.sum(p, axis=-1, keepdims=True) + jnp.exp(sink - m)
        probs.append(((p / denom).astype(_BF16), v_rows))
    outs = []
    for pn, v_rows in probs:
        o4 = jnp.dot(pn, v_rows, preferred_element_type=_F32)
        outs.append(jnp.where(lo, o4[0:CHUNK], o4[CHUNK:2 * CHUNK]))
        outs.append(jnp.where(lo, o4[2 * CHUNK:3 * CHUNK], o4[3 * CHUNK:]))
    per_chunk = 2 * KV_HEADS
    return [jnp.concatenate(outs[i * per_chunk:(i + 1) * per_chunk], axis=-1) for i in range(len(chunks))]


def _attn_prompt_kernel(sinks_ref, q_ref, kvp_ref, kvc_ref, meta_ref, bias_ref, o_ref):
    g = pl.program_id(1)
    prev = kvp_ref[0]
    cur = kvc_ref[0]
    meta = meta_ref[...]
    chunks = []
    for h in range(2):
        c = 2 * g + h
        band = jnp.concatenate([prev[h * CHUNK:], cur[:(h + 1) * CHUNK]], axis=0)
        min_valid = (N_BACK - jnp.minimum(c, N_BACK)) * CHUNK
        chunks.append((q_ref[0, h * CHUNK:(h + 1) * CHUNK, :], band[:, :KV_W], band[:, KV_W:],
                       meta[:, :KV_W], meta[:, KV_W:], bias_ref.at[jnp.minimum(c, N_BACK)], min_valid))
    for h, o in enumerate(_attend(chunks, sinks_ref)):
        o_ref[0, h * CHUNK:(h + 1) * CHUNK, :] = o


def _attention_prompt(q, kv, kv_meta, bias_tab, sinks):
    b, s, _ = q.shape
    pair = 2 * CHUNK
    assert s % pair == 0 and N_BACK == 2
    return pl.pallas_call(
        _attn_prompt_kernel,
        grid=(b, s // pair),
        in_specs=[
            pl.BlockSpec(memory_space=pltpu.SMEM),
            pl.BlockSpec((1, pair, ATT_W), lambda bi, g: (bi, g, 0)),
            pl.BlockSpec((1, pair, 2 * KV_W), lambda bi, g: (bi, jnp.maximum(g - 1, 0), 0)),
            pl.BlockSpec((1, pair, 2 * KV_W), lambda bi, g: (bi, g, 0)),
            pl.BlockSpec((N_META, 2 * KV_W), lambda bi, g: (0, 0)),
            pl.BlockSpec((3, N_HEADS, CHUNK, N_KEYS_ATT), lambda bi, g: (0, 0, 0, 0)),
        ],
        out_specs=pl.BlockSpec((1, pair, ATT_W), lambda bi, g: (bi, g, 0)),
        out_shape=jax.ShapeDtypeStruct((b, s, ATT_W), _F32),
        name="attn_prompt",
    )(sinks, q, kv, kv, kv_meta, bias_tab)


def _attn_sample_kernel(sinks_ref, q_ref, kv_ref, ck_ref, cv_ref, mk_ref, mv_ref, bias_ref, o_ref):
    kv = kv_ref[0]
    kk_band = jnp.concatenate([ck_ref[0], kv[:, :KV_W]], axis=0)
    vv_band = jnp.concatenate([cv_ref[0], kv[:, KV_W:]], axis=0)
    o_ref[0] = _attend([(q_ref[0], kk_band, vv_band, mk_ref[0], mv_ref[0], bias_ref.at[0], 0)],
                       sinks_ref)[0]


def _attention_sample(q, kv, cache_k, cache_v, meta_k, meta_v, bias_tab, sinks):
    b = q.shape[0]
    per_b = lambda bi: (bi, 0, 0)
    return pl.pallas_call(
        _attn_sample_kernel,
        grid=(b,),
        in_specs=[
            pl.BlockSpec(memory_space=pltpu.SMEM),
            pl.BlockSpec((1, CHUNK, ATT_W), per_b),
            pl.BlockSpec((1, CHUNK, 2 * KV_W), per_b),
            pl.BlockSpec((1, WINDOW, KV_W), per_b),
            pl.BlockSpec((1, WINDOW, KV_W), per_b),
            pl.BlockSpec((1, N_META, KV_W), per_b),
            pl.BlockSpec((1, N_META, KV_W), per_b),
            pl.BlockSpec((1, N_HEADS, CHUNK, N_KEYS_ATT), lambda bi: (N_BACK, 0, 0, 0)),
        ],
        out_specs=pl.BlockSpec((1, CHUNK, ATT_W), per_b),
        out_shape=jax.ShapeDtypeStruct((b, CHUNK, ATT_W), _F32),
        name="attn_sample",
    )(sinks, q, kv, cache_k, cache_v, meta_k, meta_v, bias_tab)


def _retention_kernel(gl_ref, st0_ref, rq_ref, rk_ref, rv_ref, decay_ref, qdec_ref, kdec_ref,
                      o_ref, st_ref):
    @pl.when(pl.program_id(0) == 0)
    def _():
        st_ref[...] = st0_ref[...]

    nb = rq_ref.shape[0]
    nt_dims = (((1,), (1,)), ((), ()))
    tn_dims = (((0,), (0,)), ((), ()))
    pending = []
    for b in range(nb):
        for h in range(RET_HEADS):
            sl = slice(h * RET_DK, (h + 1) * RET_DK)
            q = rq_ref[b, :, sl]
            k = rk_ref[b, :, sl]
            vb = rv_ref[b, :, sl].astype(_BF16)
            qb = q.astype(_BF16)
            s = lax.dot_general(qb, k.astype(_BF16), nt_dims, preferred_element_type=_F32)
            s = (s * decay_ref[h]).astype(_BF16)
            st = st_ref[b, h]
            cross = jnp.dot(qb, st.astype(_BF16), preferred_element_type=_F32) * qdec_ref[h]
            kd = (k * kdec_ref[h]).astype(_BF16)
            upd = lax.dot_general(kd, vb, tn_dims, preferred_element_type=_F32)
            st_ref[b, h] = gl_ref[h] * st + upd
            pending.append((s, vb, cross))
    for b in range(nb):
        outs = []
        for h in range(RET_HEADS):
            s, vb, cross = pending[b * RET_HEADS + h]
            outs.append(jnp.dot(s, vb, preferred_element_type=_F32) + cross)
        o_ref[b] = jnp.concatenate(outs, axis=-1)


def _retention_tables(chunk_len):
    log_gamma = jnp.log(1.0 - 2.0 ** (-5.0 - jnp.arange(RET_HEADS, dtype=_F32)))
    idx = jnp.arange(chunk_len, dtype=_F32)
    diff = idx[:, None] - idx[None, :]
    decay = jnp.where(diff >= 0, jnp.exp(jnp.maximum(diff, 0.0)[None] * log_gamma[:, None, None]), 0.0)
    q_dec = jnp.exp((idx + 1.0)[None, :] * log_gamma[:, None])
    k_dec = jnp.exp((chunk_len - 1.0 - idx)[None, :] * log_gamma[:, None])
    bcast = lambda t: jnp.broadcast_to(t[:, :, None], (RET_HEADS, chunk_len, RET_DK))
    g_len = jnp.exp(chunk_len * log_gamma)
    return g_len, decay, bcast(q_dec), bcast(k_dec)


def _retention(state0, rq, rk, rv, chunk_len):
    b, s, _ = rq.shape
    g_len, decay, q_dec, k_dec = _retention_tables(chunk_len)
    seq = pl.BlockSpec((b, chunk_len, RET_W), lambda c: (0, c, 0))
    whole4 = pl.BlockSpec((b, RET_HEADS, RET_DK, RET_DK), lambda c: (0, 0, 0, 0))
    tab = lambda n: pl.BlockSpec((RET_HEADS, chunk_len, n), lambda c: (0, 0, 0))
    return pl.pallas_call(
        _retention_kernel,
        grid=(s // chunk_len,),
        in_specs=[pl.BlockSpec(memory_space=pltpu.SMEM), whole4, seq, seq, seq,
                  tab(chunk_len), tab(RET_DK), tab(RET_DK)],
        out_specs=[seq, whole4],
        out_shape=[jax.ShapeDtypeStruct((b, s, RET_W), _F32),
                   jax.ShapeDtypeStruct((b, RET_HEADS, RET_DK, RET_DK), _F32)],
        compiler_params=pltpu.CompilerParams(dimension_semantics=("arbitrary",)),
        name="retention",
    )(g_len, state0, rq, rk, rv, decay, q_dec, k_dec)


def _top16_rows(x, ids=None, sentinel=None):
    if ids is None:
        ids = lax.broadcasted_iota(jnp.int32, x.shape, 0)
        sentinel = x.shape[0]
    vals, idxs = [], []
    for _ in range(PEER_TOPK):
        m = jnp.max(x, axis=0, keepdims=True)
        idx = jnp.min(jnp.where(x == m, ids, sentinel), axis=0, keepdims=True)
        vals.append(m)
        idxs.append(idx)
        x = jnp.where(ids == idx, -jnp.inf, x)
    return vals, idxs


def _top16_of_128(x):
    n_slab = N_KEYS // SUBLANES
    cols = x.shape[1]
    sub = lax.broadcasted_iota(jnp.int32, (SUBLANES, cols), 0)
    vals = [x[v * SUBLANES:(v + 1) * SUBLANES] for v in range(n_slab)]
    rows = [sub + v * SUBLANES for v in range(n_slab)]
    for p in range(n_slab):
        for a in range(p % 2, n_slab - 1, 2):
            lt = vals[a] < vals[a + 1]
            vals[a], vals[a + 1] = jnp.where(lt, vals[a + 1], vals[a]), jnp.where(lt, vals[a], vals[a + 1])
            rows[a], rows[a + 1] = jnp.where(lt, rows[a + 1], rows[a]), jnp.where(lt, rows[a], rows[a + 1])
    out_v, out_i = [], []
    for r in range(PEER_TOPK):
        m = jnp.max(vals[0], axis=0, keepdims=True)
        idx = jnp.min(jnp.where(vals[0] == m, rows[0], N_KEYS), axis=0, keepdims=True)
        out_v.append(m)
        out_i.append(idx)
        pop = rows[0] == idx
        for q in range(PEER_TOPK - 1 - r):
            vals[q] = jnp.where(pop, vals[q + 1], vals[q])
            rows[q] = jnp.where(pop, rows[q + 1], rows[q])
    return out_v, out_i


def _candidate_pairs():
    return [(a, b) for a in range(PEER_TOPK) for b in range(PEER_TOPK) if (a + 1) * (b + 1) <= PEER_TOPK]


N_CAND = 56


def _mix_route_kernel(x_ref, oatt_ref, oret_ref, rg_ref, lnin_g_ref, lnin_b_ref, gn_ref, wout_ref,
                      ln1_g_ref, ln1_b_ref, wq_ref, sk_ref, cid_ref,
                      h1_ref, eidx_ref, gate_ref, qs_ref):
    h = _layer_norm(x_ref[...], lnin_g_ref[...], lnin_b_ref[...])
    oret = oret_ref[...]
    rg = rg_ref[...]
    gn = gn_ref[...]
    ys = []
    for hh in range(RET_HEADS):
        sl = slice(hh * RET_DK, (hh + 1) * RET_DK)
        seg = oret[:, sl]
        mu = jnp.mean(seg, axis=-1, keepdims=True)
        sc = seg - mu
        var = jnp.mean(sc * sc, axis=-1, keepdims=True)
        gate = rg[:, sl]
        ys.append(sc * lax.rsqrt(var + LN_EPS) * gn[:, sl] * (gate * jax.nn.sigmoid(gate)))
    mixed = jnp.concatenate([oatt_ref[...]] + ys, axis=-1).astype(_BF16)
    mix = jnp.dot(mixed, wout_ref[...], preferred_element_type=_F32)
    h1 = _layer_norm(ALPHA * h + mix, ln1_g_ref[...], ln1_b_ref[...])
    h1_ref[...] = h1

    qp = jnp.dot(h1.astype(_BF16), wq_ref[...], preferred_element_type=_F32).astype(_BF16)
    for p in range(PEER_HEADS):
        qs_ref[p] = qp[:, p * LANES:(p + 1) * LANES]

    tm = x_ref.shape[0]
    row16 = lax.broadcasted_iota(jnp.int32, (PEER_TOPK, tm), 0)

    def head_pair(pp, carry):
        head_body(pp)
        head_body(pp + PEER_HEADS // 2)
        return carry

    def head_body(p):
        sc_t = lax.dot_general(sk_ref[p], qs_ref[p], (((1,), (1,)), ((), ())),
                               preferred_element_type=_F32)
        s1, i1 = _top16_of_128(sc_t[:N_KEYS])
        s2, i2 = _top16_of_128(sc_t[N_KEYS:])
        i1c = jnp.concatenate(i1, axis=0)
        i2c = jnp.concatenate(i2, axis=0)
        pairs = _candidate_pairs()
        pad = [jnp.full_like(s1[0], -jnp.inf)] * (N_CAND - len(pairs))
        cand = jnp.concatenate([s1[a] + s2[b] for a, b in pairs] + pad, axis=0)
        top, sel = _top16_rows(cand, cid_ref[...], PEER_TOPK * PEER_TOPK)
        e_rows = []
        for r in range(PEER_TOPK):
            a = sel[r] >> 4
            b = sel[r] & (PEER_TOPK - 1)
            e1 = jnp.sum(jnp.where(row16 == a, i1c, 0), axis=0, keepdims=True)
            e2 = jnp.sum(jnp.where(row16 == b, i2c, 0), axis=0, keepdims=True)
            e_rows.append(e1 * N_KEYS + e2)
        topc = jnp.concatenate(top, axis=0)
        ex = jnp.exp(topc - top[0])
        gate = ex / jnp.sum(ex, axis=0, keepdims=True)
        off = pl.multiple_of(p * PEER_TOPK, PEER_TOPK)
        eidx_ref[pl.ds(off, PEER_TOPK), :] = jnp.concatenate(e_rows, axis=0)
        gate_ref[pl.ds(off, PEER_TOPK), :] = gate

    lax.fori_loop(0, PEER_HEADS // 2, head_pair, 0)


def _mix_route(x2d, o_att, o_ret, rg, lnin_g, lnin_b, gn_g, w_out_bf, ln1_g, ln1_b, wq_bf, sk_blk):
    t = x2d.shape[0]
    tm = ROUTE_TM
    row = lambda i: (i, 0)
    const = lambda i: (0, 0)
    vec = pl.BlockSpec((1, D_MODEL), const)
    assert t % tm == 0
    flat = [a * PEER_TOPK + b for a, b in _candidate_pairs()]
    flat += [PEER_TOPK * PEER_TOPK] * (N_CAND - len(flat))
    cand_ids = jnp.asarray(np.broadcast_to(np.asarray(flat, np.int32)[:, None], (N_CAND, tm)))
    return pl.pallas_call(
        _mix_route_kernel,
        grid=(t // tm,),
        in_specs=[
            pl.BlockSpec((tm, D_MODEL), row),
            pl.BlockSpec((tm, ATT_W), row),
            pl.BlockSpec((tm, RET_W), row),
            pl.BlockSpec((tm, RET_W), row),
            vec, vec,
            pl.BlockSpec((1, RET_W), const),
            pl.BlockSpec((D_MODEL, D_MODEL), const),
            vec, vec,
            pl.BlockSpec((D_MODEL, D_MODEL), const),
            pl.BlockSpec((PEER_HEADS, 2 * N_KEYS, LANES), lambda i: (0, 0, 0)),
            pl.BlockSpec((N_CAND, tm), const),
        ],
        out_specs=[
            pl.BlockSpec((tm, D_MODEL), row),
            pl.BlockSpec((N_SEL, tm), lambda i: (0, i)),
            pl.BlockSpec((N_SEL, tm), lambda i: (0, i)),
        ],
        out_shape=[
            jax.ShapeDtypeStruct((t, D_MODEL), _F32),
            jax.ShapeDtypeStruct((N_SEL, t), jnp.int32),
            jax.ShapeDtypeStruct((N_SEL, t), _F32),
        ],
        scratch_shapes=[pltpu.VMEM((PEER_HEADS, tm, LANES), _BF16)],
        compiler_params=_vmem_limit(40 * 2 ** 20),
        name="mix_route",
    )(x2d, o_att, o_ret, rg, lnin_g, lnin_b, gn_g, w_out_bf, ln1_g, ln1_b, wq_bf, sk_blk, cand_ids)


PACK_ROWS = 512


def _pack_uv_kernel(u_ref, v_ref, o_ref, stage_ref):
    def group(g, carry):
        r0 = pl.multiple_of(g * SUBLANES, SUBLANES)
        for t, src in enumerate((u_ref, v_ref)):
            blk = src[pl.ds(r0, SUBLANES), :]
            for s in range(SUBLANES):
                stage_ref[t, s * SUBLANES:(s + 1) * SUBLANES, :] = blk[:, s * LANES:(s + 1) * LANES]
            for r in range(SUBLANES):
                o_ref[r0 + r, t] = stage_ref[t, pl.ds(r, SUBLANES, stride=SUBLANES), :]
        return carry

    lax.fori_loop(0, u_ref.shape[0] // SUBLANES, group, 0)


def _pack_uv(u_tab, v_tab):
    n = u_tab.shape[0]
    assert n % PACK_ROWS == 0
    row = lambda i: (i, 0)
    return pl.pallas_call(
        _pack_uv_kernel,
        grid=(n // PACK_ROWS,),
        in_specs=[pl.BlockSpec((PACK_ROWS, D_MODEL), row), pl.BlockSpec((PACK_ROWS, D_MODEL), row)],
        out_specs=pl.BlockSpec((PACK_ROWS, 2) + ROW_TILE, lambda i: (i, 0, 0, 0)),
        out_shape=jax.ShapeDtypeStruct((n, 2) + ROW_TILE, _F32),
        scratch_shapes=[pltpu.VMEM((2, SUBLANES * SUBLANES, LANES), _F32)],
        name="pack_uv",
    )(u_tab, v_tab)


_BITREV8 = (0, 4, 2, 6, 1, 5, 3, 7)


def _rows_to_sublanes(tiles, sub):
    m4 = sub < 4
    m2 = (sub & 2) == 0
    m1 = (sub & 1) == 0
    t = [tiles[i] for i in _BITREV8]
    c = []
    for a, b in ((0, 1), (2, 3), (4, 5), (6, 7)):
        w = jnp.where(m4, t[a], t[b])
        x = jnp.where(m4, t[b], t[a])
        c.append(w + pltpu.roll(x, 4, axis=0))
    d = []
    for a, b in ((0, 1), (2, 3)):
        d.append(jnp.where(m2, c[a] + pltpu.roll(c[a], 6, axis=0), c[b] + pltpu.roll(c[b], 2, axis=0)))
    return jnp.where(m1, d[0] + pltpu.roll(d[0], 7, axis=0), d[1] + pltpu.roll(d[1], 1, axis=0))


def _peer_apply_kernel(idx_hbm, h1_ref, gate_ref, ln2_g_ref, ln2_b_ref, uv_hbm,
                       y_ref, uvbuf, idx_smem, gsem, isem, xs_ref, wb_ref, o_buf, *, n_tiles):
    tb, nt = PEER_TB, PEER_NT
    step_tok = tb * nt
    n_steps = n_tiles // nt
    j = pl.program_id(0)
    issue_on = j < n_steps
    comp_on = j >= 1
    lane0 = ((j - 1) * step_tok) % LANES

    def idx_copy(tile, buf):
        return pltpu.make_async_copy(idx_hbm.at[tile], idx_smem.at[buf], isem.at[buf])

    @pl.when(j == 0)
    def _():
        for i in range(nt):
            idx_copy(i, i).start()

    @pl.when(comp_on)
    def _():
        h1 = h1_ref[...]
        for s in range(SUBLANES):
            xs_ref[s * step_tok:(s + 1) * step_tok, :] = h1[:, s * LANES:(s + 1) * LANES]

    def tile_pass(i, do_issue, do_comp):
        n_groups = N_SEL // SUBLANES
        tile_new = nt * j + i
        islot = tile_new % PEER_SLOTS
        cslot = (tile_new - nt) % PEER_SLOTS
        if do_issue:
            idx_copy(tile_new, i).wait()
        if do_comp:
            pltpu.make_async_copy(uv_hbm.at[pl.ds(0, tb * N_SEL)], uvbuf.at[cslot], gsem.at[cslot]).wait()

        def make_issue(tok):
            pending = list(range(N_SEL)) if do_issue else []

            def issue(n):
                for _ in range(min(n, len(pending))):
                    k = pending.pop(0)
                    e = idx_smem[i, tok, k]
                    pltpu.make_async_copy(uv_hbm.at[e], uvbuf.at[islot, tok * N_SEL + k],
                                          gsem.at[islot]).start(priority=k % 2)
            return issue

        def refill():
            @pl.when(tile_new + nt < n_tiles)
            def _():
                idx_copy(tile_new + nt, i).start()

        if not do_comp:
            def issue_body(tok, carry):
                make_issue(tok)(N_SEL)
                return carry
            lax.fori_loop(0, tb, issue_body, 0)
            refill()
            return

        issues = [make_issue(tok) for tok in range(tb)]
        sub = lax.broadcasted_iota(jnp.int32, ROW_TILE, 0)

        lane = lax.broadcasted_iota(jnp.int32, (N_SEL, LANES), 1)
        act = jnp.zeros((N_SEL, LANES), _F32)
        for tok in range(tb):
            xt = xs_ref[pl.ds(i * tb + tok, SUBLANES, stride=step_tok), :]
            parts = []

            def load_u(g):
                return [uvbuf[cslot, tok * N_SEL + g * SUBLANES + r, 0] for r in range(SUBLANES)]

            for g0 in range(0, n_groups, PEER_REGION):
                region = [load_u(g) for g in range(g0, g0 + PEER_REGION)]
                issues[tok](PEER_ISSUE_U * PEER_REGION)
                for rows in region:
                    prods = [row * xt for row in rows]
                    parts.append(jnp.sum(_rows_to_sublanes(prods, sub), axis=1, keepdims=True))
            act = jnp.where(lane == tok, jnp.concatenate(parts, axis=0), act)

        gate0 = pltpu.roll(gate_ref[...], (2 * LANES - lane0 - i * tb) % LANES, axis=1)
        w = jax.nn.gelu(act) * gate0
        for tok in range(tb):
            wb_ref[:, tok * LANES:(tok + 1) * LANES] = jnp.broadcast_to(w[:, tok:tok + 1], (N_SEL, LANES))

        for tok in range(tb):
            accs = [jnp.zeros(ROW_TILE, _F32) for _ in range(4)]

            def load_v(k0):
                ks = range(k0, k0 + SUBLANES)
                return ([jnp.broadcast_to(wb_ref[k:k + 1, tok * LANES:(tok + 1) * LANES], ROW_TILE) for k in ks],
                        [uvbuf[cslot, tok * N_SEL + k, 1] for k in ks])

            for k0 in range(0, N_SEL, 2 * SUBLANES):
                pair = [load_v(k0), load_v(k0 + SUBLANES)]
                issues[tok](2 * (N_SEL // SUBLANES - PEER_ISSUE_U))
                for wks, vks in pair:
                    for r, (wk, vk) in enumerate(zip(wks, vks)):
                        accs[r % 4] = accs[r % 4] + wk * vk
            issues[tok](N_SEL)
            ltok = i * tb + tok
            o_buf[ltok * SUBLANES:(ltok + 1) * SUBLANES, :] = (accs[0] + accs[1]) + (accs[2] + accs[3])
        if do_issue:
            refill()

    for i in range(nt):
        pl.when(jnp.logical_and(issue_on, jnp.logical_not(comp_on)))(functools.partial(tile_pass, i, True, False))
        pl.when(jnp.logical_and(issue_on, comp_on))(functools.partial(tile_pass, i, True, True))
        pl.when(jnp.logical_and(jnp.logical_not(issue_on), comp_on))(functools.partial(tile_pass, i, False, True))

    @pl.when(comp_on)
    def _():
        out = jnp.concatenate([o_buf[pl.ds(s, step_tok, stride=SUBLANES), :] for s in range(SUBLANES)],
                              axis=-1)
        y_ref[...] = _layer_norm(ALPHA * h1_ref[...] + out, ln2_g_ref[...], ln2_b_ref[...])


def _peer_apply(h1, eidx_t, gate_t, uv_tab, ln2_g, ln2_b):
    t = h1.shape[0]
    tb, nt = PEER_TB, PEER_NT
    step_tok = tb * nt
    assert t % step_tok == 0 and LANES % step_tok == 0 and PEER_SLOTS > nt
    n_tiles = t // tb
    idx_tiles = eidx_t.T.reshape(n_tiles, tb, N_SEL)
    done = lambda j: jnp.maximum(j - 1, 0)
    const = lambda j: (0, 0)
    kern = functools.partial(_peer_apply_kernel, n_tiles=n_tiles)
    return pl.pallas_call(
        kern,
        grid=(n_tiles // nt + 1,),
        in_specs=[
            pl.BlockSpec(memory_space=pl.ANY),
            pl.BlockSpec((step_tok, D_MODEL), lambda j: (done(j), 0)),
            pl.BlockSpec((N_SEL, LANES), lambda j: (0, done(j) * step_tok // LANES)),
            pl.BlockSpec((1, D_MODEL), const),
            pl.BlockSpec((1, D_MODEL), const),
            pl.BlockSpec(memory_space=pl.ANY),
        ],
        out_specs=pl.BlockSpec((step_tok, D_MODEL), lambda j: (done(j), 0)),
        out_shape=jax.ShapeDtypeStruct((t, D_MODEL), _F32),
        scratch_shapes=[
            pltpu.VMEM((PEER_SLOTS, tb * N_SEL, 2) + ROW_TILE, _F32),
            pltpu.SMEM((nt, tb, N_SEL), jnp.int32),
            pltpu.SemaphoreType.DMA((PEER_SLOTS,)),
            pltpu.SemaphoreType.DMA((nt,)),
            pltpu.VMEM((SUBLANES * step_tok, LANES), _F32),
            pltpu.VMEM((N_SEL, tb * LANES), _F32),
            pltpu.VMEM((step_tok * SUBLANES, LANES), _F32),
        ],
        compiler_params=pltpu.CompilerParams(
            dimension_semantics=("arbitrary",),
            vmem_limit_bytes=int(PEER_SLOTS * tb * N_SEL * 2 * 4096 + 12 * 2 ** 20)),
        name="peer_apply",
    )(idx_tiles, h1, gate_t, ln2_g, ln2_b, uv_tab)


def _rope_tables(pos):
    half = RET_DK // 2
    inv = ROPE_BASE ** (-jnp.arange(half, dtype=_F32) / half)
    ang = jnp.asarray(pos, _F32)[:, None] * inv[None, :]
    cos = jnp.cos(ang)
    sin = jnp.sin(ang)
    return jnp.concatenate([cos, cos], axis=-1), jnp.concatenate([-sin, sin], axis=-1)


def _subkey_blocks(sub_keys):
    z = jnp.zeros((PEER_HEADS, N_KEYS, PEER_DK_HALF), sub_keys.dtype)
    top = jnp.concatenate([sub_keys[:, 0], z], axis=-1)
    bot = jnp.concatenate([z, sub_keys[:, 1]], axis=-1)
    return jnp.concatenate([top, bot], axis=1).astype(_BF16)


def kernel(x_prompt, x_sample, cache_meta_k, cache_meta_v, cache_swa_k, cache_swa_v, state_ret,
           meta_tokens, ln_in_g, ln_in_b, rel_bias, w_in, w_out, attn_sinks, ret_gn_g, ln1_g, ln1_b,
           peer_wq, peer_subkeys, peer_u, peer_v, ln2_g, ln2_b):
    b, s, _ = x_prompt.shape
    db, ds, _ = x_sample.shape
    row = lambda a: a.reshape(1, -1)

    w_in_bf = w_in[0].astype(_BF16)
    w_out_bf = w_out[0].astype(_BF16)
    wq_bf = peer_wq[0].astype(_BF16)
    sk_blk = _subkey_blocks(peer_subkeys[0])
    uv_tab = _pack_uv(peer_u[0], peer_v[0])
    lnin_g, lnin_b = row(ln_in_g), row(ln_in_b)
    cos_m, sin_m = _rope_tables(np.arange(N_META))
    cos_p, sin_p = _rope_tables(N_META + np.arange(s))
    cos_s, sin_s = _rope_tables(N_META + PAST_LEN + np.arange(ds))
    bias_tab = _rel_bias_tables(rel_bias)
    sinks = attn_sinks[0]

    _, kv_m, rq_m, rk_m, rv_m, _ = _ln_proj(meta_tokens.astype(x_prompt.dtype), N_META, lnin_g, lnin_b,
                                         w_in_bf, cos_m, sin_m, N_META)
    xp2 = x_prompt.reshape(b * s, D_MODEL)
    xs2 = x_sample.reshape(db * ds, D_MODEL)
    q_p, kv_p, rq_p, rk_p, rv_p, rg_p = _ln_proj(xp2, s, lnin_g, lnin_b, w_in_bf, cos_p, sin_p, PROJ_TM)
    q_s, kv_s, rq_s, rk_s, rv_s, rg_s = _ln_proj(xs2, ds, lnin_g, lnin_b, w_in_bf, cos_s, sin_s, ds)

    kv_p3 = kv_p.reshape(b, s, 2 * KV_W)
    o_att_p = _attention_prompt(q_p.reshape(b, s, ATT_W), kv_p3, kv_m, bias_tab, sinks)
    kv_s3 = kv_s.reshape(db, ds, 2 * KV_W)
    o_att_s = _attention_sample(
        q_s.reshape(db, ds, ATT_W), kv_s3,
        cache_swa_k[0].reshape(db, WINDOW, KV_W), cache_swa_v[0].reshape(db, WINDOW, KV_W),
        cache_meta_k[0].reshape(db, N_META, KV_W), cache_meta_v[0].reshape(db, N_META, KV_W),
        bias_tab, sinks)

    zero_state = jnp.zeros((1, RET_HEADS, RET_DK, RET_DK), _F32)
    _, st_meta = _retention(zero_state, rq_m[None], rk_m[None], rv_m[None], N_META)
    st0_p = jnp.broadcast_to(st_meta, (b, RET_HEADS, RET_DK, RET_DK))
    seq3 = lambda a, n, l: a.reshape(n, l, RET_W)
    o_ret_p, st_p = _retention(st0_p, seq3(rq_p, b, s), seq3(rk_p, b, s), seq3(rv_p, b, s), CHUNK)
    o_ret_s, st_s = _retention(state_ret[0].astype(_F32), seq3(rq_s, db, ds), seq3(rk_s, db, ds),
                               seq3(rv_s, db, ds), ds)

    def tail(x2d, o_att, o_ret, rg):
        h1, eidx_t, gate_t = _mix_route(x2d, o_att, o_ret, rg, lnin_g, lnin_b, row(ret_gn_g[0]),
                                        w_out_bf, row(ln1_g[0]), row(ln1_b[0]), wq_bf, sk_blk)
        return _peer_apply(h1, eidx_t, gate_t, uv_tab, row(ln2_g[0]), row(ln2_b[0]))

    y_p = tail(xp2, o_att_p.reshape(b * s, ATT_W), o_ret_p.reshape(b * s, RET_W), rg_p)
    y_s = tail(xs2, o_att_s.reshape(db * ds, ATT_W), o_ret_s.reshape(db * ds, RET_W), rg_s)

    kvh = lambda a, n, l: a.reshape(1, n, l, KV_HEADS, HEAD_DIM)
    meta_k = jnp.broadcast_to(kvh(kv_m[:, :KV_W], 1, N_META), (1, b, N_META, KV_HEADS, HEAD_DIM))
    meta_v = jnp.broadcast_to(kvh(kv_m[:, KV_W:], 1, N_META), (1, b, N_META, KV_HEADS, HEAD_DIM))
    tail_kv = kv_p3[:, s - WINDOW:]
    return (y_p.reshape(b, s, D_MODEL), y_s.reshape(db, ds, D_MODEL),
            meta_k, meta_v,
            kvh(tail_kv[..., :KV_W], b, WINDOW), kvh(tail_kv[..., KV_W:], b, WINDOW),
            st_p[None],
            kvh(kv_s3[..., :KV_W], db, ds), kvh(kv_s3[..., KV_W:], db, ds),
            st_s[None])
```

```python
import functools
import math

import jax
import jax.numpy as jnp
import numpy as np
from jax import lax
from jax.experimental import pallas as pl
from jax.experimental.pallas import tpu as pltpu

D_MODEL = 1024
CHUNK = 64
N_META = 16
PAST_LEN = 2048
ATT_W = 512
HEAD_DIM = 64
N_HEADS = 8
KV_HEADS = 2
GQA_GROUP = N_HEADS // KV_HEADS
KV_W = KV_HEADS * HEAD_DIM
WINDOW = 128
N_BACK = WINDOW // CHUNK
BAND = (N_BACK + 1) * CHUNK
N_KEYS_ATT = BAND + N_META
N_BUCKETS = 32
MAX_DISTANCE = 128
RET_W = 512
RET_HEADS = 4
RET_DK = 128
ROPE_BASE = 10000.0
PROJ_W = 2816
N_KEYS = 128
N_EXPERTS = N_KEYS * N_KEYS
PEER_HEADS = 8
PEER_TOPK = 16
PEER_DK_HALF = 64
N_SEL = PEER_HEADS * PEER_TOPK
ALPHA = 2.0 ** 0.25
LN_EPS = 1e-5
NEG_INF = -1e30

LANES = 128
SUBLANES = 8
ROW_TILE = (SUBLANES, LANES)

PEER_TB = 8
PEER_NT = 2
PEER_SLOTS = 3
PEER_REGION = 4
PEER_ISSUE_U = 5
ATTN_G = 4
ROUTE_TM = 256
ROUTE_HEADS_PER_ITER = 4
PROJ_TM = 256

_F32 = jnp.float32
_BF16 = jnp.bfloat16


def _vmem_limit(nbytes):
    return pltpu.CompilerParams(vmem_limit_bytes=int(nbytes))


def _layer_norm(x, g, b):
    mu = jnp.mean(x, axis=-1, keepdims=True)
    xc = x - mu
    var = jnp.mean(xc * xc, axis=-1, keepdims=True)
    return xc * lax.rsqrt(var + LN_EPS) * g + b


def _ln_proj_kernel(x_ref, g_ref, b_ref, w_ref, cos_ref, sin_ref,
                    q_ref, kv_ref, rq_ref, rk_ref, rv_ref, rg_ref):
    h = _layer_norm(x_ref[...], g_ref[...], b_ref[...])
    p = jnp.dot(h.astype(_BF16), w_ref[...], preferred_element_type=_F32)
    cos = cos_ref[...]
    sin = sin_ref[...]

    def rotary(a):
        outs = []
        for hh in range(RET_HEADS):
            seg = a[:, hh * RET_DK:(hh + 1) * RET_DK]
            outs.append(seg * cos + pltpu.roll(seg, RET_DK // 2, axis=1) * sin)
        return jnp.concatenate(outs, axis=-1)

    q_ref[...] = p[:, 0:512]
    kv_ref[...] = p[:, 512:768]
    rq_ref[...] = rotary(p[:, 768:1280])
    rk_ref[...] = rotary(p[:, 1280:1792]) * (RET_DK ** -0.5)
    rv_ref[...] = p[:, 1792:2304]
    rg_ref[...] = p[:, 2304:2816]


def _ln_proj(x2d, seq_len, ln_g, ln_b, w_in_bf, cos, sin, tm):
    t = x2d.shape[0]
    nblk_s = seq_len // tm
    row = lambda i: (i, 0)
    const = lambda i: (0, 0)
    outs = [jax.ShapeDtypeStruct((t, w), _F32) for w in (512, 256, 512, 512, 512, 512)]
    return pl.pallas_call(
        _ln_proj_kernel,
        grid=(t // tm,),
        in_specs=[
            pl.BlockSpec((tm, D_MODEL), row),
            pl.BlockSpec((1, D_MODEL), const),
            pl.BlockSpec((1, D_MODEL), const),
            pl.BlockSpec((D_MODEL, PROJ_W), const),
            pl.BlockSpec((tm, RET_DK), lambda i: (i % nblk_s, 0)),
            pl.BlockSpec((tm, RET_DK), lambda i: (i % nblk_s, 0)),
        ],
        out_specs=[pl.BlockSpec((tm, w), row) for w in (512, 256, 512, 512, 512, 512)],
        out_shape=outs,
        compiler_params=_vmem_limit(48 * 2 ** 20),
        name="ln_proj",
    )(x2d, ln_g, ln_b, w_in_bf, cos, sin)


def _t5_bucket_np(rel):
    nb = N_BUCKETS // 2
    max_exact = nb // 2
    n = np.abs(rel)
    large = max_exact + (np.log(np.maximum(n, max_exact).astype(np.float32) / max_exact)
                         / math.log(MAX_DISTANCE / max_exact) * (nb - max_exact)).astype(np.int32)
    large = np.minimum(large, nb - 1)
    return (np.where(rel > 0, nb, 0) + np.where(n < max_exact, n, large)).astype(np.int32)


def _bias_buckets():
    i = np.arange(CHUNK)
    jb = np.arange(BAND)
    m = np.arange(N_META)
    rel_band = jb[None, :] - N_BACK * CHUNK - i[:, None]
    out = []
    for c in range(3):
        rel_meta = m[None, :] - (N_META + c * CHUNK + i[:, None])
        out.append(np.concatenate([rel_band, rel_meta], axis=-1))
    return _t5_bucket_np(np.stack(out))


def _rel_bias_kernel(rb_ref, bucket_ref, o_ref):
    h = pl.program_id(1)
    bucket = bucket_ref[0]
    acc = jnp.zeros(bucket.shape, _F32)
    for v in range(N_BUCKETS):
        acc = jnp.where(bucket == v, rb_ref[v, h], acc)
    o_ref[0, 0] = acc


def _rel_bias_tables(rel_bias):
    buckets = jnp.asarray(_bias_buckets())
    return pl.pallas_call(
        _rel_bias_kernel,
        grid=(3, N_HEADS),
        in_specs=[
            pl.BlockSpec(memory_space=pltpu.SMEM),
            pl.BlockSpec((1, CHUNK, N_KEYS_ATT), lambda v, h: (v, 0, 0)),
        ],
        out_specs=pl.BlockSpec((1, 1, CHUNK, N_KEYS_ATT), lambda v, h: (v, h, 0, 0)),
        out_shape=jax.ShapeDtypeStruct((3, N_HEADS, CHUNK, N_KEYS_ATT), _F32),
        name="rel_bias",
    )(rel_bias, buckets)


def _attend(chunks, sinks_ref):
    lane = lax.broadcasted_iota(jnp.int32, (1, LANES), 1)
    lo = lane < HEAD_DIM
    rows4 = GQA_GROUP * CHUNK
    col = lax.broadcasted_iota(jnp.int32, (rows4, N_KEYS_ATT), 1)
    row = lax.broadcasted_iota(jnp.int32, (rows4, 1), 0)
    jobs = []
    for q, kk_band, vv_band, kk_meta, vv_meta, bias_ref, min_valid_col in chunks:
        kk = jnp.concatenate([kk_band, kk_meta], axis=0)
        vv = jnp.concatenate([vv_band, vv_meta], axis=0)
        kk_r = pltpu.roll(kk, HEAD_DIM, axis=1)
        vv_r = pltpu.roll(vv, HEAD_DIM, axis=1)
        k_dup = [jnp.where(lo, kk, kk_r).astype(_BF16), jnp.where(lo, kk_r, kk).astype(_BF16)]
        v_dup = [jnp.where(lo, vv, vv_r).astype(_BF16), jnp.where(lo, vv_r, vv).astype(_BF16)]
        valid = col >= min_valid_col
        for kvh in range(KV_HEADS):
            qa = q[:, (2 * kvh) * LANES:(2 * kvh + 1) * LANES]
            qb = q[:, (2 * kvh + 1) * LANES:(2 * kvh + 2) * LANES]
            lhs = jnp.concatenate([jnp.where(lo, qa, 0.0), jnp.where(lo, 0.0, qa),
                                   jnp.where(lo, qb, 0.0), jnp.where(lo, 0.0, qb)], axis=0).astype(_BF16)
            s = lax.dot_general(lhs, k_dup[kvh], (((1,), (1,)), ((), ())), preferred_element_type=_F32)
            jobs.append((s, kvh, bias_ref, valid, v_dup[kvh]))
    probs = []
    for s, kvh, bias_ref, valid, v_rows in jobs:
        heads = [GQA_GROUP * kvh + i for i in range(GQA_GROUP)]
        s = s * (HEAD_DIM ** -0.5) + jnp.concatenate([bias_ref[h] for h in heads], axis=0)
        s = jnp.where(valid, s, NEG_INF)
        sink = jnp.where(row < CHUNK, sinks_ref[heads[0]],
                         jnp.where(row < 2 * CHUNK, sinks_ref[heads[1]],
                                   jnp.where(row < 3 * CHUNK, sinks_ref[heads[2]], sinks_ref[heads[3]])))
        m = jnp.maximum(jnp.max(s, axis=-1, keepdims=True), sink)
        p = jnp.exp(s - m)
        denom = jnp.sum(p, axis=-1, keepdims=True) + jnp.exp(sink - m)
        probs.append(((p / denom).astype(_BF16), v_rows))
    outs = []
    for pn, v_rows in probs:
        o4 = jnp.dot(pn, v_rows, preferred_element_type=_F32)
        outs.append(jnp.where(lo, o4[0:CHUNK], o4[CHUNK:2 * CHUNK]))
        outs.append(jnp.where(lo, o4[2 * CHUNK:3 * CHUNK], o4[3 * CHUNK:]))
    per_chunk = 2 * KV_HEADS
    return [jnp.concatenate(outs[i * per_chunk:(i + 1) * per_chunk], axis=-1) for i in range(len(chunks))]


def _attn_prompt_kernel(sinks_ref, q_ref, kvp_ref, kvc_ref, meta_ref, bias_ref, o_ref):
    g = pl.program_id(1)
    frames = jnp.concatenate([kvp_ref[0, (ATTN_G - N_BACK) * CHUNK:], kvc_ref[0]], axis=0)
    meta = meta_ref[...]
    chunks = []
    for h in range(ATTN_G):
        c = ATTN_G * g + h
        band = frames[h * CHUNK:h * CHUNK + BAND]
        min_valid = (N_BACK - jnp.minimum(c, N_BACK)) * CHUNK
        chunks.append((q_ref[0, h * CHUNK:(h + 1) * CHUNK, :], band[:, :KV_W], band[:, KV_W:],
                       meta[:, :KV_W], meta[:, KV_W:], bias_ref.at[jnp.minimum(c, N_BACK)], min_valid))
    for h, o in enumerate(_attend(chunks, sinks_ref)):
        o_ref[0, h * CHUNK:(h + 1) * CHUNK, :] = o


def _attention_prompt(q, kv, kv_meta, bias_tab, sinks):
    b, s, _ = q.shape
    pair = ATTN_G * CHUNK
    assert s % pair == 0 and ATTN_G >= N_BACK
    return pl.pallas_call(
        _attn_prompt_kernel,
        grid=(b, s // pair),
        in_specs=[
            pl.BlockSpec(memory_space=pltpu.SMEM),
            pl.BlockSpec((1, pair, ATT_W), lambda bi, g: (bi, g, 0)),
            pl.BlockSpec((1, pair, 2 * KV_W), lambda bi, g: (bi, jnp.maximum(g - 1, 0), 0)),
            pl.BlockSpec((1, pair, 2 * KV_W), lambda bi, g: (bi, g, 0)),
            pl.BlockSpec((N_META, 2 * KV_W), lambda bi, g: (0, 0)),
            pl.BlockSpec((3, N_HEADS, CHUNK, N_KEYS_ATT), lambda bi, g: (0, 0, 0, 0)),
        ],
        out_specs=pl.BlockSpec((1, pair, ATT_W), lambda bi, g: (bi, g, 0)),
        out_shape=jax.ShapeDtypeStruct((b, s, ATT_W), _F32),
        name="attn_prompt",
    )(sinks, q, kv, kv, kv_meta, bias_tab)


def _attn_sample_kernel(sinks_ref, q_ref, kv_ref, ck_ref, cv_ref, mk_ref, mv_ref, bias_ref, o_ref):
    kv = kv_ref[0]
    kk_band = jnp.concatenate([ck_ref[0], kv[:, :KV_W]], axis=0)
    vv_band = jnp.concatenate([cv_ref[0], kv[:, KV_W:]], axis=0)
    o_ref[0] = _attend([(q_ref[0], kk_band, vv_band, mk_ref[0], mv_ref[0], bias_ref.at[0], 0)],
                       sinks_ref)[0]


def _attention_sample(q, kv, cache_k, cache_v, meta_k, meta_v, bias_tab, sinks):
    b = q.shape[0]
    per_b = lambda bi: (bi, 0, 0)
    return pl.pallas_call(
        _attn_sample_kernel,
        grid=(b,),
        in_specs=[
            pl.BlockSpec(memory_space=pltpu.SMEM),
            pl.BlockSpec((1, CHUNK, ATT_W), per_b),
            pl.BlockSpec((1, CHUNK, 2 * KV_W), per_b),
            pl.BlockSpec((1, WINDOW, KV_W), per_b),
            pl.BlockSpec((1, WINDOW, KV_W), per_b),
            pl.BlockSpec((1, N_META, KV_W), per_b),
            pl.BlockSpec((1, N_META, KV_W), per_b),
            pl.BlockSpec((1, N_HEADS, CHUNK, N_KEYS_ATT), lambda bi: (N_BACK, 0, 0, 0)),
        ],
        out_specs=pl.BlockSpec((1, CHUNK, ATT_W), per_b),
        out_shape=jax.ShapeDtypeStruct((b, CHUNK, ATT_W), _F32),
        name="attn_sample",
    )(sinks, q, kv, cache_k, cache_v, meta_k, meta_v, bias_tab)


def _retention_kernel(gl_ref, st0_ref, rq_ref, rk_ref, rv_ref, decay_ref, qdec_ref, kdec_ref,
                      o_ref, st_ref):
    @pl.when(pl.program_id(0) == 0)
    def _():
        st_ref[...] = st0_ref[...]

    nb = rq_ref.shape[0]
    nt_dims = (((1,), (1,)), ((), ()))
    tn_dims = (((0,), (0,)), ((), ()))
    pending = []
    for b in range(nb):
        for h in range(RET_HEADS):
            sl = slice(h * RET_DK, (h + 1) * RET_DK)
            q = rq_ref[b, :, sl]
            k = rk_ref[b, :, sl]
            vb = rv_ref[b, :, sl].astype(_BF16)
            qb = q.astype(_BF16)
            s = lax.dot_general(qb, k.astype(_BF16), nt_dims, preferred_element_type=_F32)
            s = (s * decay_ref[h]).astype(_BF16)
            st = st_ref[b, h]
            cross = jnp.dot(qb, st.astype(_BF16), preferred_element_type=_F32) * qdec_ref[h]
            kd = (k * kdec_ref[h]).astype(_BF16)
            upd = lax.dot_general(kd, vb, tn_dims, preferred_element_type=_F32)
            st_ref[b, h] = gl_ref[h] * st + upd
            pending.append((s, vb, cross))
    for b in range(nb):
        outs = []
        for h in range(RET_HEADS):
            s, vb, cross = pending[b * RET_HEADS + h]
            outs.append(jnp.dot(s, vb, preferred_element_type=_F32) + cross)
        o_ref[b] = jnp.concatenate(outs, axis=-1)


def _retention_tables(chunk_len):
    log_gamma = jnp.log(1.0 - 2.0 ** (-5.0 - jnp.arange(RET_HEADS, dtype=_F32)))
    idx = jnp.arange(chunk_len, dtype=_F32)
    diff = idx[:, None] - idx[None, :]
    decay = jnp.where(diff >= 0, jnp.exp(jnp.maximum(diff, 0.0)[None] * log_gamma[:, None, None]), 0.0)
    q_dec = jnp.exp((idx + 1.0)[None, :] * log_gamma[:, None])
    k_dec = jnp.exp((chunk_len - 1.0 - idx)[None, :] * log_gamma[:, None])
    bcast = lambda t: jnp.broadcast_to(t[:, :, None], (RET_HEADS, chunk_len, RET_DK))
    g_len = jnp.exp(chunk_len * log_gamma)
    return g_len, decay, bcast(q_dec), bcast(k_dec)


def _retention(state0, rq, rk, rv, chunk_len):
    b, s, _ = rq.shape
    g_len, decay, q_dec, k_dec = _retention_tables(chunk_len)
    seq = pl.BlockSpec((b, chunk_len, RET_W), lambda c: (0, c, 0))
    whole4 = pl.BlockSpec((b, RET_HEADS, RET_DK, RET_DK), lambda c: (0, 0, 0, 0))
    tab = lambda n: pl.BlockSpec((RET_HEADS, chunk_len, n), lambda c: (0, 0, 0))
    return pl.pallas_call(
        _retention_kernel,
        grid=(s // chunk_len,),
        in_specs=[pl.BlockSpec(memory_space=pltpu.SMEM), whole4, seq, seq, seq,
                  tab(chunk_len), tab(RET_DK), tab(RET_DK)],
        out_specs=[seq, whole4],
        out_shape=[jax.ShapeDtypeStruct((b, s, RET_W), _F32),
                   jax.ShapeDtypeStruct((b, RET_HEADS, RET_DK, RET_DK), _F32)],
        compiler_params=pltpu.CompilerParams(dimension_semantics=("arbitrary",)),
        name="retention",
    )(g_len, state0, rq, rk, rv, decay, q_dec, k_dec)


def _top16_rows(x, ids=None, sentinel=None):
    if ids is None:
        ids = lax.broadcasted_iota(jnp.int32, x.shape, 0)
        sentinel = x.shape[0]
    vals, idxs = [], []
    for _ in range(PEER_TOPK):
        m = jnp.max(x, axis=0, keepdims=True)
        idx = jnp.min(jnp.where(x == m, ids, sentinel), axis=0, keepdims=True)
        vals.append(m)
        idxs.append(idx)
        x = jnp.where(ids == idx, -jnp.inf, x)
    return vals, idxs


def _top16_of_128(x):
    n_slab = N_KEYS // SUBLANES
    cols = x.shape[1]
    sub = lax.broadcasted_iota(jnp.int32, (SUBLANES, cols), 0)
    vals = [x[v * SUBLANES:(v + 1) * SUBLANES] for v in range(n_slab)]
    rows = [sub + v * SUBLANES for v in range(n_slab)]
    for p in range(n_slab):
        for a in range(p % 2, n_slab - 1, 2):
            lt = vals[a] < vals[a + 1]
            vals[a], vals[a + 1] = jnp.where(lt, vals[a + 1], vals[a]), jnp.where(lt, vals[a], vals[a + 1])
            rows[a], rows[a + 1] = jnp.where(lt, rows[a + 1], rows[a]), jnp.where(lt, rows[a], rows[a + 1])
    out_v, out_i = [], []
    for r in range(PEER_TOPK):
        m = jnp.max(vals[0], axis=0, keepdims=True)
        idx = jnp.min(jnp.where(vals[0] == m, rows[0], N_KEYS), axis=0, keepdims=True)
        out_v.append(m)
        out_i.append(idx)
        pop = rows[0] == idx
        for q in range(PEER_TOPK - 1 - r):
            vals[q] = jnp.where(pop, vals[q + 1], vals[q])
            rows[q] = jnp.where(pop, rows[q + 1], rows[q])
    return out_v, out_i


def _candidate_pairs():
    return [(a, b) for a in range(PEER_TOPK) for b in range(PEER_TOPK) if (a + 1) * (b + 1) <= PEER_TOPK]


N_CAND = 56


def _mix_route_kernel(x_ref, oatt_ref, oret_ref, rg_ref, lnin_g_ref, lnin_b_ref, gn_ref, wout_ref,
                      ln1_g_ref, ln1_b_ref, wq_ref, sk_ref, cid_ref,
                      h1_ref, eidx_ref, gate_ref, qs_ref):
    h = _layer_norm(x_ref[...], lnin_g_ref[...], lnin_b_ref[...])
    oret = oret_ref[...]
    rg = rg_ref[...]
    gn = gn_ref[...]
    ys = []
    for hh in range(RET_HEADS):
        sl = slice(hh * RET_DK, (hh + 1) * RET_DK)
        seg = oret[:, sl]
        mu = jnp.mean(seg, axis=-1, keepdims=True)
        sc = seg - mu
        var = jnp.mean(sc * sc, axis=-1, keepdims=True)
        gate = rg[:, sl]
        ys.append(sc * lax.rsqrt(var + LN_EPS) * gn[:, sl] * (gate * jax.nn.sigmoid(gate)))
    mixed = jnp.concatenate([oatt_ref[...]] + ys, axis=-1).astype(_BF16)
    mix = jnp.dot(mixed, wout_ref[...], preferred_element_type=_F32)
    h1 = _layer_norm(ALPHA * h + mix, ln1_g_ref[...], ln1_b_ref[...])
    h1_ref[...] = h1

    qp = jnp.dot(h1.astype(_BF16), wq_ref[...], preferred_element_type=_F32).astype(_BF16)
    for p in range(PEER_HEADS):
        qs_ref[p] = qp[:, p * LANES:(p + 1) * LANES]

    def head_pair(pp, carry):
        for i in range(ROUTE_HEADS_PER_ITER):
            head_body(pp + i * (PEER_HEADS // ROUTE_HEADS_PER_ITER))
        return carry

    def head_body(p):
        sc_t = lax.dot_general(sk_ref[p], qs_ref[p], (((1,), (1,)), ((), ())),
                               preferred_element_type=_F32)
        s1, i1 = _top16_of_128(sc_t[:N_KEYS])
        s2, i2 = _top16_of_128(sc_t[N_KEYS:])
        pairs = _candidate_pairs()
        pad = [jnp.full_like(s1[0], -jnp.inf)] * (N_CAND - len(pairs))
        cand = jnp.concatenate([s1[a] + s2[b] for a, b in pairs] + pad, axis=0)
        i1s = [i * N_KEYS for i in i1]
        cand_e = jnp.concatenate([i1s[a] + i2[b] for a, b in pairs] + [jnp.zeros_like(i1[0])] * len(pad),
                                 axis=0)
        cid = cid_ref[...]
        top, sel = _top16_rows(cand, cid, PEER_TOPK * PEER_TOPK)
        e_rows = [jnp.sum(jnp.where(cid == sel[r], cand_e, 0), axis=0, keepdims=True)
                  for r in range(PEER_TOPK)]
        topc = jnp.concatenate(top, axis=0)
        ex = jnp.exp(topc - top[0])
        gate = ex / jnp.sum(ex, axis=0, keepdims=True)
        off = pl.multiple_of(p * PEER_TOPK, PEER_TOPK)
        eidx_ref[pl.ds(off, PEER_TOPK), :] = jnp.concatenate(e_rows, axis=0)
        gate_ref[pl.ds(off, PEER_TOPK), :] = gate

    lax.fori_loop(0, PEER_HEADS // ROUTE_HEADS_PER_ITER, head_pair, 0)


def _mix_route(x2d, o_att, o_ret, rg, lnin_g, lnin_b, gn_g, w_out_bf, ln1_g, ln1_b, wq_bf, sk_blk):
    t = x2d.shape[0]
    tm = ROUTE_TM
    row = lambda i: (i, 0)
    const = lambda i: (0, 0)
    vec = pl.BlockSpec((1, D_MODEL), const)
    assert t % tm == 0
    flat = [a * PEER_TOPK + b for a, b in _candidate_pairs()]
    flat += [PEER_TOPK * PEER_TOPK] * (N_CAND - len(flat))
    cand_ids = jnp.asarray(np.broadcast_to(np.asarray(flat, np.int32)[:, None], (N_CAND, tm)))
    return pl.pallas_call(
        _mix_route_kernel,
        grid=(t // tm,),
        in_specs=[
            pl.BlockSpec((tm, D_MODEL), row),
            pl.BlockSpec((tm, ATT_W), row),
            pl.BlockSpec((tm, RET_W), row),
            pl.BlockSpec((tm, RET_W), row),
            vec, vec,
            pl.BlockSpec((1, RET_W), const),
            pl.BlockSpec((D_MODEL, D_MODEL), const),
            vec, vec,
            pl.BlockSpec((D_MODEL, D_MODEL), const),
            pl.BlockSpec((PEER_HEADS, 2 * N_KEYS, LANES), lambda i: (0, 0, 0)),
            pl.BlockSpec((N_CAND, tm), const),
        ],
        out_specs=[
            pl.BlockSpec((tm, D_MODEL), row),
            pl.BlockSpec((N_SEL, tm), lambda i: (0, i)),
            pl.BlockSpec((N_SEL, tm), lambda i: (0, i)),
        ],
        out_shape=[
            jax.ShapeDtypeStruct((t, D_MODEL), _F32),
            jax.ShapeDtypeStruct((N_SEL, t), jnp.int32),
            jax.ShapeDtypeStruct((N_SEL, t), _F32),
        ],
        scratch_shapes=[pltpu.VMEM((PEER_HEADS, tm, LANES), _BF16)],
        compiler_params=_vmem_limit(40 * 2 ** 20),
        name="mix_route",
    )(x2d, o_att, o_ret, rg, lnin_g, lnin_b, gn_g, w_out_bf, ln1_g, ln1_b, wq_bf, sk_blk, cand_ids)


PACK_ROWS = 512


def _pack_uv_kernel(u_ref, v_ref, o_ref, stage_ref):
    def group(g, carry):
        r0 = pl.multiple_of(g * SUBLANES, SUBLANES)
        for t, src in enumerate((u_ref, v_ref)):
            blk = src[pl.ds(r0, SUBLANES), :]
            for s in range(SUBLANES):
                stage_ref[t, s * SUBLANES:(s + 1) * SUBLANES, :] = blk[:, s * LANES:(s + 1) * LANES]
            for r in range(SUBLANES):
                o_ref[r0 + r, t] = stage_ref[t, pl.ds(r, SUBLANES, stride=SUBLANES), :]
        return carry

    lax.fori_loop(0, u_ref.shape[0] // SUBLANES, group, 0)


def _pack_uv(u_tab, v_tab):
    n = u_tab.shape[0]
    assert n % PACK_ROWS == 0
    row = lambda i: (i, 0)
    return pl.pallas_call(
        _pack_uv_kernel,
        grid=(n // PACK_ROWS,),
        in_specs=[pl.BlockSpec((PACK_ROWS, D_MODEL), row), pl.BlockSpec((PACK_ROWS, D_MODEL), row)],
        out_specs=pl.BlockSpec((PACK_ROWS, 2) + ROW_TILE, lambda i: (i, 0, 0, 0)),
        out_shape=jax.ShapeDtypeStruct((n, 2) + ROW_TILE, _F32),
        scratch_shapes=[pltpu.VMEM((2, SUBLANES * SUBLANES, LANES), _F32)],
        name="pack_uv",
    )(u_tab, v_tab)


_BITREV8 = (0, 4, 2, 6, 1, 5, 3, 7)


def _rows_to_sublanes(tiles, sub):
    m4 = sub < 4
    m2 = (sub & 2) == 0
    m1 = (sub & 1) == 0
    t = [tiles[i] for i in _BITREV8]
    c = []
    for a, b in ((0, 1), (2, 3), (4, 5), (6, 7)):
        w = jnp.where(m4, t[a], t[b])
        x = jnp.where(m4, t[b], t[a])
        c.append(w + pltpu.roll(x, 4, axis=0))
    d = []
    for a, b in ((0, 1), (2, 3)):
        d.append(jnp.where(m2, c[a] + pltpu.roll(c[a], 6, axis=0), c[b] + pltpu.roll(c[b], 2, axis=0)))
    return jnp.where(m1, d[0] + pltpu.roll(d[0], 7, axis=0), d[1] + pltpu.roll(d[1], 1, axis=0))


def _peer_apply_kernel(idx_hbm, h1_ref, gate_ref, ln2_g_ref, ln2_b_ref, uv_hbm,
                       y_ref, uvbuf, idx_smem, gsem, isem, xs_ref, wb_ref, o_buf, *, n_tiles):
    tb, nt = PEER_TB, PEER_NT
    step_tok = tb * nt
    n_steps = n_tiles // nt
    j = pl.program_id(0)
    issue_on = j < n_steps
    comp_on = j >= 1
    lane0 = ((j - 1) * step_tok) % LANES

    def idx_copy(tile, buf):
        return pltpu.make_async_copy(idx_hbm.at[tile], idx_smem.at[buf], isem.at[buf])

    @pl.when(j == 0)
    def _():
        for i in range(nt):
            idx_copy(i, i).start()

    @pl.when(comp_on)
    def _():
        h1 = h1_ref[...]
        for s in range(SUBLANES):
            xs_ref[s * step_tok:(s + 1) * step_tok, :] = h1[:, s * LANES:(s + 1) * LANES]

    def tile_pass(i, do_issue, do_comp):
        n_groups = N_SEL // SUBLANES
        tile_new = nt * j + i
        islot = tile_new % PEER_SLOTS
        cslot = (tile_new - nt) % PEER_SLOTS
        if do_issue:
            idx_copy(tile_new, i).wait()
        if do_comp:
            pltpu.make_async_copy(uv_hbm.at[pl.ds(0, tb * N_SEL)], uvbuf.at[cslot], gsem.at[cslot]).wait()

        def make_issue(tok):
            pending = list(range(N_SEL)) if do_issue else []

            def issue(n):
                for _ in range(min(n, len(pending))):
                    k = pending.pop(0)
                    e = idx_smem[i, tok, k]
                    pltpu.make_async_copy(uv_hbm.at[e], uvbuf.at[islot, tok * N_SEL + k],
                                          gsem.at[islot]).start(priority=k % 2)
            return issue

        def refill():
            @pl.when(tile_new + nt < n_tiles)
            def _():
                idx_copy(tile_new + nt, i).start()

        if not do_comp:
            def issue_body(tok, carry):
                make_issue(tok)(N_SEL)
                return carry
            lax.fori_loop(0, tb, issue_body, 0)
            refill()
            return

        issues = [make_issue(tok) for tok in range(tb)]
        sub = lax.broadcasted_iota(jnp.int32, ROW_TILE, 0)

        lane = lax.broadcasted_iota(jnp.int32, (N_SEL, LANES), 1)
        act = jnp.zeros((N_SEL, LANES), _F32)
        for tok in range(tb):
            xt = xs_ref[pl.ds(i * tb + tok, SUBLANES, stride=step_tok), :]
            parts = []

            def load_u(g):
                return [uvbuf[cslot, tok * N_SEL + g * SUBLANES + r, 0] for r in range(SUBLANES)]

            for g0 in range(0, n_groups, PEER_REGION):
                region = [load_u(g) for g in range(g0, g0 + PEER_REGION)]
                issues[tok](PEER_ISSUE_U * PEER_REGION)
                for rows in region:
                    prods = [row * xt for row in rows]
                    parts.append(jnp.sum(_rows_to_sublanes(prods, sub), axis=1, keepdims=True))
            act = jnp.where(lane == tok, jnp.concatenate(parts, axis=0), act)

        gate0 = pltpu.roll(gate_ref[...], (2 * LANES - lane0 - i * tb) % LANES, axis=1)
        w = jax.nn.gelu(act) * gate0
        for tok in range(tb):
            wb_ref[:, tok * LANES:(tok + 1) * LANES] = jnp.broadcast_to(w[:, tok:tok + 1], (N_SEL, LANES))

        for tok in range(tb):
            accs = [jnp.zeros(ROW_TILE, _F32) for _ in range(4)]

            def load_v(k0):
                ks = range(k0, k0 + SUBLANES)
                return ([jnp.broadcast_to(wb_ref[k:k + 1, tok * LANES:(tok + 1) * LANES], ROW_TILE) for k in ks],
                        [uvbuf[cslot, tok * N_SEL + k, 1] for k in ks])

            for k0 in range(0, N_SEL, 2 * SUBLANES):
                pair = [load_v(k0), load_v(k0 + SUBLANES)]
                issues[tok](2 * (N_SEL // SUBLANES - PEER_ISSUE_U))
                for wks, vks in pair:
                    for r, (wk, vk) in enumerate(zip(wks, vks)):
                        accs[r % 4] = accs[r % 4] + wk * vk
            issues[tok](N_SEL)
            ltok = i * tb + tok
            o_buf[ltok * SUBLANES:(ltok + 1) * SUBLANES, :] = (accs[0] + accs[1]) + (accs[2] + accs[3])
        if do_issue:
            refill()

    for i in range(nt):
        pl.when(jnp.logical_and(issue_on, jnp.logical_not(comp_on)))(functools.partial(tile_pass, i, True, False))
        pl.when(jnp.logical_and(issue_on, comp_on))(functools.partial(tile_pass, i, True, True))
        pl.when(jnp.logical_and(jnp.logical_not(issue_on), comp_on))(functools.partial(tile_pass, i, False, True))

    @pl.when(comp_on)
    def _():
        out = jnp.concatenate([o_buf[pl.ds(s, step_tok, stride=SUBLANES), :] for s in range(SUBLANES)],
                              axis=-1)
        y_ref[...] = _layer_norm(ALPHA * h1_ref[...] + out, ln2_g_ref[...], ln2_b_ref[...])


def _peer_apply(h1, eidx_t, gate_t, uv_tab, ln2_g, ln2_b):
    t = h1.shape[0]
    tb, nt = PEER_TB, PEER_NT
    step_tok = tb * nt
    assert t % step_tok == 0 and LANES % step_tok == 0 and PEER_SLOTS > nt
    n_tiles = t // tb
    idx_tiles = eidx_t.T.reshape(n_tiles, tb, N_SEL)
    done = lambda j: jnp.maximum(j - 1, 0)
    const = lambda j: (0, 0)
    kern = functools.partial(_peer_apply_kernel, n_tiles=n_tiles)
    return pl.pallas_call(
        kern,
        grid=(n_tiles // nt + 1,),
        in_specs=[
            pl.BlockSpec(memory_space=pl.ANY),
            pl.BlockSpec((step_tok, D_MODEL), lambda j: (done(j), 0)),
            pl.BlockSpec((N_SEL, LANES), lambda j: (0, done(j) * step_tok // LANES)),
            pl.BlockSpec((1, D_MODEL), const),
            pl.BlockSpec((1, D_MODEL), const),
            pl.BlockSpec(memory_space=pl.ANY),
        ],
        out_specs=pl.BlockSpec((step_tok, D_MODEL), lambda j: (done(j), 0)),
        out_shape=jax.ShapeDtypeStruct((t, D_MODEL), _F32),
        scratch_shapes=[
            pltpu.VMEM((PEER_SLOTS, tb * N_SEL, 2) + ROW_TILE, _F32),
            pltpu.SMEM((nt, tb, N_SEL), jnp.int32),
            pltpu.SemaphoreType.DMA((PEER_SLOTS,)),
            pltpu.SemaphoreType.DMA((nt,)),
            pltpu.VMEM((SUBLANES * step_tok, LANES), _F32),
            pltpu.VMEM((N_SEL, tb * LANES), _F32),
            pltpu.VMEM((step_tok * SUBLANES, LANES), _F32),
        ],
        compiler_params=pltpu.CompilerParams(
            dimension_semantics=("arbitrary",),
            vmem_limit_bytes=int(PEER_SLOTS * tb * N_SEL * 2 * 4096 + 12 * 2 ** 20)),
        name="peer_apply",
    )(idx_tiles, h1, gate_t, ln2_g, ln2_b, uv_tab)


def _rope_tables(pos):
    half = RET_DK // 2
    inv = ROPE_BASE ** (-jnp.arange(half, dtype=_F32) / half)
    ang = jnp.asarray(pos, _F32)[:, None] * inv[None, :]
    cos = jnp.cos(ang)
    sin = jnp.sin(ang)
    return jnp.concatenate([cos, cos], axis=-1), jnp.concatenate([-sin, sin], axis=-1)


def _subkey_blocks(sub_keys):
    z = jnp.zeros((PEER_HEADS, N_KEYS, PEER_DK_HALF), sub_keys.dtype)
    top = jnp.concatenate([sub_keys[:, 0], z], axis=-1)
    bot = jnp.concatenate([z, sub_keys[:, 1]], axis=-1)
    return jnp.concatenate([top, bot], axis=1).astype(_BF16)


def kernel(x_prompt, x_sample, cache_meta_k, cache_meta_v, cache_swa_k, cache_swa_v, state_ret,
           meta_tokens, ln_in_g, ln_in_b, rel_bias, w_in, w_out, attn_sinks, ret_gn_g, ln1_g, ln1_b,
           peer_wq, peer_subkeys, peer_u, peer_v, ln2_g, ln2_b):
    b, s, _ = x_prompt.shape
    db, ds, _ = x_sample.shape
    row = lambda a: a.reshape(1, -1)

    w_in_bf = w_in[0].astype(_BF16)
    w_out_bf = w_out[0].astype(_BF16)
    wq_bf = peer_wq[0].astype(_BF16)
    sk_blk = _subkey_blocks(peer_subkeys[0])
    uv_tab = _pack_uv(peer_u[0], peer_v[0])
    lnin_g, lnin_b = row(ln_in_g), row(ln_in_b)
    cos_m, sin_m = _rope_tables(np.arange(N_META))
    cos_p, sin_p = _rope_tables(N_META + np.arange(s))
    cos_s, sin_s = _rope_tables(N_META + PAST_LEN + np.arange(ds))
    bias_tab = _rel_bias_tables(rel_bias)
    sinks = attn_sinks[0]

    _, kv_m, rq_m, rk_m, rv_m, _ = _ln_proj(meta_tokens.astype(x_prompt.dtype), N_META, lnin_g, lnin_b,
                                         w_in_bf, cos_m, sin_m, N_META)
    xp2 = x_prompt.reshape(b * s, D_MODEL)
    xs2 = x_sample.reshape(db * ds, D_MODEL)
    q_p, kv_p, rq_p, rk_p, rv_p, rg_p = _ln_proj(xp2, s, lnin_g, lnin_b, w_in_bf, cos_p, sin_p, PROJ_TM)
    q_s, kv_s, rq_s, rk_s, rv_s, rg_s = _ln_proj(xs2, ds, lnin_g, lnin_b, w_in_bf, cos_s, sin_s, ds)

    kv_p3 = kv_p.reshape(b, s, 2 * KV_W)
    o_att_p = _attention_prompt(q_p.reshape(b, s, ATT_W), kv_p3, kv_m, bias_tab, sinks)
    kv_s3 = kv_s.reshape(db, ds, 2 * KV_W)
    o_att_s = _attention_sample(
        q_s.reshape(db, ds, ATT_W), kv_s3,
        cache_swa_k[0].reshape(db, WINDOW, KV_W), cache_swa_v[0].reshape(db, WINDOW, KV_W),
        cache_meta_k[0].reshape(db, N_META, KV_W), cache_meta_v[0].reshape(db, N_META, KV_W),
        bias_tab, sinks)

    zero_state = jnp.zeros((1, RET_HEADS, RET_DK, RET_DK), _F32)
    _, st_meta = _retention(zero_state, rq_m[None], rk_m[None], rv_m[None], N_META)
    st0_p = jnp.broadcast_to(st_meta, (b, RET_HEADS, RET_DK, RET_DK))
    seq3 = lambda a, n, l: a.reshape(n, l, RET_W)
    o_ret_p, st_p = _retention(st0_p, seq3(rq_p, b, s), seq3(rk_p, b, s), seq3(rv_p, b, s), CHUNK)
    o_ret_s, st_s = _retention(state_ret[0].astype(_F32), seq3(rq_s, db, ds), seq3(rk_s, db, ds),
                               seq3(rv_s, db, ds), ds)

    def tail(x2d, o_att, o_ret, rg):
        h1, eidx_t, gate_t = _mix_route(x2d, o_att, o_ret, rg, lnin_g, lnin_b, row(ret_gn_g[0]),
                                        w_out_bf, row(ln1_g[0]), row(ln1_b[0]), wq_bf, sk_blk)
        return _peer_apply(h1, eidx_t, gate_t, uv_tab, row(ln2_g[0]), row(ln2_b[0]))

    y_p = tail(xp2, o_att_p.reshape(b * s, ATT_W), o_ret_p.reshape(b * s, RET_W), rg_p)
    y_s = tail(xs2, o_att_s.reshape(db * ds, ATT_W), o_ret_s.reshape(db * ds, RET_W), rg_s)

    kvh = lambda a, n, l: a.reshape(1, n, l, KV_HEADS, HEAD_DIM)
    meta_k = jnp.broadcast_to(kvh(kv_m[:, :KV_W], 1, N_META), (1, b, N_META, KV_HEADS, HEAD_DIM))
    meta_v = jnp.broadcast_to(kvh(kv_m[:, KV_W:], 1, N_META), (1, b, N_META, KV_HEADS, HEAD_DIM))
    tail_kv = kv_p3[:, s - WINDOW:]
    return (y_p.reshape(b, s, D_MODEL), y_s.reshape(db, ds, D_MODEL),
            meta_k, meta_v,
            kvh(tail_kv[..., :KV_W], b, WINDOW), kvh(tail_kv[..., KV_W:], b, WINDOW),
            st_p[None],
            kvh(kv_s3[..., :KV_W], db, ds), kvh(kv_s3[..., KV_W:], db, ds),
            st_s[None])
```

```python
import functools
import math

import jax
import jax.numpy as jnp
import numpy as np
from jax import lax
from jax.experimental import pallas as pl
from jax.experimental.pallas import tpu as pltpu

D_MODEL = 1024
CHUNK = 64
N_META = 16
PAST_LEN = 2048
ATT_W = 512
HEAD_DIM = 64
N_HEADS = 8
KV_HEADS = 2
GQA_GROUP = N_HEADS // KV_HEADS
KV_W = KV_HEADS * HEAD_DIM
WINDOW = 128
N_BACK = WINDOW // CHUNK
BAND = (N_BACK + 1) * CHUNK
N_KEYS_ATT = BAND + N_META
N_BUCKETS = 32
MAX_DISTANCE = 128
RET_W = 512
RET_HEADS = 4
RET_DK = 128
ROPE_BASE = 10000.0
PROJ_W = 2816
N_KEYS = 128
N_EXPERTS = N_KEYS * N_KEYS
PEER_HEADS = 8
PEER_TOPK = 16
PEER_DK_HALF = 64
N_SEL = PEER_HEADS * PEER_TOPK
ALPHA = 2.0 ** 0.25
LN_EPS = 1e-5
NEG_INF = -1e30

LANES = 128
SUBLANES = 8
ROW_TILE = (SUBLANES, LANES)

PEER_TB = 8
PEER_NT = 2
PEER_SLOTS = 3
PEER_REGION = 4
PEER_ISSUE_U = 5
ATTN_G = 4
ROUTE_TM = 256
ROUTE_HEADS_PER_ITER = 4
PROJ_TM = 256

_F32 = jnp.float32
_BF16 = jnp.bfloat16


def _vmem_limit(nbytes):
    return pltpu.CompilerParams(vmem_limit_bytes=int(nbytes))


def _layer_norm(x, g, b):
    mu = jnp.mean(x, axis=-1, keepdims=True)
    xc = x - mu
    var = jnp.mean(xc * xc, axis=-1, keepdims=True)
    return xc * lax.rsqrt(var + LN_EPS) * g + b


def _ln_proj_kernel(x_ref, g_ref, b_ref, w_ref, cos_ref, sin_ref,
                    q_ref, kv_ref, rq_ref, rk_ref, rv_ref, rg_ref):
    h = _layer_norm(x_ref[...], g_ref[...], b_ref[...])
    p = jnp.dot(h.astype(_BF16), w_ref[...], preferred_element_type=_F32)
    cos = cos_ref[...]
    sin = sin_ref[...]

    def rotary(a):
        outs = []
        for hh in range(RET_HEADS):
            seg = a[:, hh * RET_DK:(hh + 1) * RET_DK]
            outs.append(seg * cos + pltpu.roll(seg, RET_DK // 2, axis=1) * sin)
        return jnp.concatenate(outs, axis=-1)

    q_ref[...] = p[:, 0:512]
    kv_ref[...] = p[:, 512:768]
    rq_ref[...] = rotary(p[:, 768:1280])
    rk_ref[...] = rotary(p[:, 1280:1792]) * (RET_DK ** -0.5)
    rv_ref[...] = p[:, 1792:2304]
    rg_ref[...] = p[:, 2304:2816]


def _ln_proj(x2d, seq_len, ln_g, ln_b, w_in_bf, cos, sin, tm):
    t = x2d.shape[0]
    nblk_s = seq_len // tm
    row = lambda i: (i, 0)
    const = lambda i: (0, 0)
    outs = [jax.ShapeDtypeStruct((t, w), _F32) for w in (512, 256, 512, 512, 512, 512)]
    return pl.pallas_call(
        _ln_proj_kernel,
        grid=(t // tm,),
        in_specs=[
            pl.BlockSpec((tm, D_MODEL), row),
            pl.BlockSpec((1, D_MODEL), const),
            pl.BlockSpec((1, D_MODEL), const),
            pl.BlockSpec((D_MODEL, PROJ_W), const),
            pl.BlockSpec((tm, RET_DK), lambda i: (i % nblk_s, 0)),
            pl.BlockSpec((tm, RET_DK), lambda i: (i % nblk_s, 0)),
        ],
        out_specs=[pl.BlockSpec((tm, w), row) for w in (512, 256, 512, 512, 512, 512)],
        out_shape=outs,
        compiler_params=_vmem_limit(48 * 2 ** 20),
        name="ln_proj",
    )(x2d, ln_g, ln_b, w_in_bf, cos, sin)


def _t5_bucket_np(rel):
    nb = N_BUCKETS // 2
    max_exact = nb // 2
    n = np.abs(rel)
    large = max_exact + (np.log(np.maximum(n, max_exact).astype(np.float32) / max_exact)
                         / math.log(MAX_DISTANCE / max_exact) * (nb - max_exact)).astype(np.int32)
    large = np.minimum(large, nb - 1)
    return (np.where(rel > 0, nb, 0) + np.where(n < max_exact, n, large)).astype(np.int32)


def _bias_buckets():
    i = np.arange(CHUNK)
    jb = np.arange(BAND)
    m = np.arange(N_META)
    rel_band = jb[None, :] - N_BACK * CHUNK - i[:, None]
    out = []
    for c in range(3):
        rel_meta = m[None, :] - (N_META + c * CHUNK + i[:, None])
        out.append(np.concatenate([rel_band, rel_meta], axis=-1))
    return _t5_bucket_np(np.stack(out))


def _rel_bias_kernel(rb_ref, bucket_ref, o_ref):
    h = pl.program_id(1)
    bucket = bucket_ref[0]
    acc = jnp.zeros(bucket.shape, _F32)
    for v in range(N_BUCKETS):
        acc = jnp.where(bucket == v, rb_ref[v, h], acc)
    o_ref[0, 0] = acc


def _rel_bias_tables(rel_bias):
    buckets = jnp.asarray(_bias_buckets())
    return pl.pallas_call(
        _rel_bias_kernel,
        grid=(3, N_HEADS),
        in_specs=[
            pl.BlockSpec(memory_space=pltpu.SMEM),
            pl.BlockSpec((1, CHUNK, N_KEYS_ATT), lambda v, h: (v, 0, 0)),
        ],
        out_specs=pl.BlockSpec((1, 1, CHUNK, N_KEYS_ATT), lambda v, h: (v, h, 0, 0)),
        out_shape=jax.ShapeDtypeStruct((3, N_HEADS, CHUNK, N_KEYS_ATT), _F32),
        name="rel_bias",
    )(rel_bias, buckets)


def _attend(chunks, sinks_ref):
    lane = lax.broadcasted_iota(jnp.int32, (1, LANES), 1)
    lo = lane < HEAD_DIM
    rows4 = GQA_GROUP * CHUNK
    col = lax.broadcasted_iota(jnp.int32, (rows4, N_KEYS_ATT), 1)
    row = lax.broadcasted_iota(jnp.int32, (rows4, 1), 0)
    jobs = []
    for q, kk_band, vv_band, kk_meta, vv_meta, bias_ref, min_valid_col in chunks:
        kk = jnp.concatenate([kk_band, kk_meta], axis=0)
        vv = jnp.concatenate([vv_band, vv_meta], axis=0)
        kk_r = pltpu.roll(kk, HEAD_DIM, axis=1)
        vv_r = pltpu.roll(vv, HEAD_DIM, axis=1)
        k_dup = [jnp.where(lo, kk, kk_r).astype(_BF16), jnp.where(lo, kk_r, kk).astype(_BF16)]
        v_dup = [jnp.where(lo, vv, vv_r).astype(_BF16), jnp.where(lo, vv_r, vv).astype(_BF16)]
        valid = col >= min_valid_col
        for kvh in range(KV_HEADS):
            qa = q[:, (2 * kvh) * LANES:(2 * kvh + 1) * LANES]
            qb = q[:, (2 * kvh + 1) * LANES:(2 * kvh + 2) * LANES]
            lhs = jnp.concatenate([jnp.where(lo, qa, 0.0), jnp.where(lo, 0.0, qa),
                                   jnp.where(lo, qb, 0.0), jnp.where(lo, 0.0, qb)], axis=0).astype(_BF16)
            s = lax.dot_general(lhs, k_dup[kvh], (((1,), (1,)), ((), ())), preferred_element_type=_F32)
            jobs.append((s, kvh, bias_ref, valid, v_dup[kvh]))
    probs = []
    for s, kvh, bias_ref, valid, v_rows in jobs:
        heads = [GQA_GROUP * kvh + i for i in range(GQA_GROUP)]
        s = s * (HEAD_DIM ** -0.5) + jnp.concatenate([bias_ref[h] for h in heads], axis=0)
        s = jnp.where(valid, s, NEG_INF)
        sink = jnp.where(row < CHUNK, sinks_ref[heads[0]],
                         jnp.where(row < 2 * CHUNK, sinks_ref[heads[1]],
                                   jnp.where(row < 3 * CHUNK, sinks_ref[heads[2]], sinks_ref[heads[3]])))
        m = jnp.maximum(jnp.max(s, axis=-1, keepdims=True), sink)
        p = jnp.exp(s - m)
        denom = jnp.sum(p, axis=-1, keepdims=True) + jnp.exp(sink - m)
        probs.append(((p / denom).astype(_BF16), v_rows))
    outs = []
    for pn, v_rows in probs:
        o4 = jnp.dot(pn, v_rows, preferred_element_type=_F32)
        outs.append(jnp.where(lo, o4[0:CHUNK], o4[CHUNK:2 * CHUNK]))
        outs.append(jnp.where(lo, o4[2 * CHUNK:3 * CHUNK], o4[3 * CHUNK:]))
    per_chunk = 2 * KV_HEADS
    return [jnp.concatenate(outs[i * per_chunk:(i + 1) * per_chunk], axis=-1) for i in range(len(chunks))]


def _attn_prompt_kernel(sinks_ref, q_ref, kvp_ref, kvc_ref, meta_ref, bias_ref, o_ref):
    g = pl.program_id(1)
    frames = jnp.concatenate([kvp_ref[0, (ATTN_G - N_BACK) * CHUNK:], kvc_ref[0]], axis=0)
    meta = meta_ref[...]
    chunks = []
    for h in range(ATTN_G):
        c = ATTN_G * g + h
        band = frames[h * CHUNK:h * CHUNK + BAND]
        min_valid = (N_BACK - jnp.minimum(c, N_BACK)) * CHUNK
        chunks.append((q_ref[0, h * CHUNK:(h + 1) * CHUNK, :], band[:, :KV_W], band[:, KV_W:],
                       meta[:, :KV_W], meta[:, KV_W:], bias_ref.at[jnp.minimum(c, N_BACK)], min_valid))
    for h, o in enumerate(_attend(chunks, sinks_ref)):
        o_ref[0, h * CHUNK:(h + 1) * CHUNK, :] = o


def _attention_prompt(q, kv, kv_meta, bias_tab, sinks):
    b, s, _ = q.shape
    pair = ATTN_G * CHUNK
    assert s % pair == 0 and ATTN_G >= N_BACK
    return pl.pallas_call(
        _attn_prompt_kernel,
        grid=(b, s // pair),
        in_specs=[
            pl.BlockSpec(memory_space=pltpu.SMEM),
            pl.BlockSpec((1, pair, ATT_W), lambda bi, g: (bi, g, 0)),
            pl.BlockSpec((1, pair, 2 * KV_W), lambda bi, g: (bi, jnp.maximum(g - 1, 0), 0)),
            pl.BlockSpec((1, pair, 2 * KV_W), lambda bi, g: (bi, g, 0)),
            pl.BlockSpec((N_META, 2 * KV_W), lambda bi, g: (0, 0)),
            pl.BlockSpec((3, N_HEADS, CHUNK, N_KEYS_ATT), lambda bi, g: (0, 0, 0, 0)),
        ],
        out_specs=pl.BlockSpec((1, pair, ATT_W), lambda bi, g: (bi, g, 0)),
        out_shape=jax.ShapeDtypeStruct((b, s, ATT_W), _F32),
        name="attn_prompt",
    )(sinks, q, kv, kv, kv_meta, bias_tab)


def _attn_sample_kernel(sinks_ref, q_ref, kv_ref, ck_ref, cv_ref, mk_ref, mv_ref, bias_ref, o_ref):
    kv = kv_ref[0]
    kk_band = jnp.concatenate([ck_ref[0], kv[:, :KV_W]], axis=0)
    vv_band = jnp.concatenate([cv_ref[0], kv[:, KV_W:]], axis=0)
    o_ref[0] = _attend([(q_ref[0], kk_band, vv_band, mk_ref[0], mv_ref[0], bias_ref.at[0], 0)],
                       sinks_ref)[0]


def _attention_sample(q, kv, cache_k, cache_v, meta_k, meta_v, bias_tab, sinks):
    b = q.shape[0]
    per_b = lambda bi: (bi, 0, 0)
    return pl.pallas_call(
        _attn_sample_kernel,
        grid=(b,),
        in_specs=[
            pl.BlockSpec(memory_space=pltpu.SMEM),
            pl.BlockSpec((1, CHUNK, ATT_W), per_b),
            pl.BlockSpec((1, CHUNK, 2 * KV_W), per_b),
            pl.BlockSpec((1, WINDOW, KV_W), per_b),
            pl.BlockSpec((1, WINDOW, KV_W), per_b),
            pl.BlockSpec((1, N_META, KV_W), per_b),
            pl.BlockSpec((1, N_META, KV_W), per_b),
            pl.BlockSpec((1, N_HEADS, CHUNK, N_KEYS_ATT), lambda bi: (N_BACK, 0, 0, 0)),
        ],
        out_specs=pl.BlockSpec((1, CHUNK, ATT_W), per_b),
        out_shape=jax.ShapeDtypeStruct((b, CHUNK, ATT_W), _F32),
        name="attn_sample",
    )(sinks, q, kv, cache_k, cache_v, meta_k, meta_v, bias_tab)


def _retention_kernel(gl_ref, st0_ref, rq_ref, rk_ref, rv_ref, decay_ref, qdec_ref, kdec_ref,
                      o_ref, st_ref):
    @pl.when(pl.program_id(0) == 0)
    def _():
        st_ref[...] = st0_ref[...]

    nb = rq_ref.shape[0]
    nt_dims = (((1,), (1,)), ((), ()))
    tn_dims = (((0,), (0,)), ((), ()))
    pending = []
    for b in range(nb):
        for h in range(RET_HEADS):
            sl = slice(h * RET_DK, (h + 1) * RET_DK)
            q = rq_ref[b, :, sl]
            k = rk_ref[b, :, sl]
            vb = rv_ref[b, :, sl].astype(_BF16)
            qb = q.astype(_BF16)
            s = lax.dot_general(qb, k.astype(_BF16), nt_dims, preferred_element_type=_F32)
            s = (s * decay_ref[h]).astype(_BF16)
            st = st_ref[b, h]
            cross = jnp.dot(qb, st.astype(_BF16), preferred_element_type=_F32) * qdec_ref[h]
            kd = (k * kdec_ref[h]).astype(_BF16)
            upd = lax.dot_general(kd, vb, tn_dims, preferred_element_type=_F32)
            st_ref[b, h] = gl_ref[h] * st + upd
            pending.append((s, vb, cross))
    for b in range(nb):
        outs = []
        for h in range(RET_HEADS):
            s, vb, cross = pending[b * RET_HEADS + h]
            outs.append(jnp.dot(s, vb, preferred_element_type=_F32) + cross)
        o_ref[b] = jnp.concatenate(outs, axis=-1)


def _retention_tables(chunk_len):
    log_gamma = jnp.log(1.0 - 2.0 ** (-5.0 - jnp.arange(RET_HEADS, dtype=_F32)))
    idx = jnp.arange(chunk_len, dtype=_F32)
    diff = idx[:, None] - idx[None, :]
    decay = jnp.where(diff >= 0, jnp.exp(jnp.maximum(diff, 0.0)[None] * log_gamma[:, None, None]), 0.0)
    q_dec = jnp.exp((idx + 1.0)[None, :] * log_gamma[:, None])
    k_dec = jnp.exp((chunk_len - 1.0 - idx)[None, :] * log_gamma[:, None])
    bcast = lambda t: jnp.broadcast_to(t[:, :, None], (RET_HEADS, chunk_len, RET_DK))
    g_len = jnp.exp(chunk_len * log_gamma)
    return g_len, decay, bcast(q_dec), bcast(k_dec)


def _retention(state0, rq, rk, rv, chunk_len):
    b, s, _ = rq.shape
    g_len, decay, q_dec, k_dec = _retention_tables(chunk_len)
    seq = pl.BlockSpec((b, chunk_len, RET_W), lambda c: (0, c, 0))
    whole4 = pl.BlockSpec((b, RET_HEADS, RET_DK, RET_DK), lambda c: (0, 0, 0, 0))
    tab = lambda n: pl.BlockSpec((RET_HEADS, chunk_len, n), lambda c: (0, 0, 0))
    return pl.pallas_call(
        _retention_kernel,
        grid=(s // chunk_len,),
        in_specs=[pl.BlockSpec(memory_space=pltpu.SMEM), whole4, seq, seq, seq,
                  tab(chunk_len), tab(RET_DK), tab(RET_DK)],
        out_specs=[seq, whole4],
        out_shape=[jax.ShapeDtypeStruct((b, s, RET_W), _F32),
                   jax.ShapeDtypeStruct((b, RET_HEADS, RET_DK, RET_DK), _F32)],
        compiler_params=pltpu.CompilerParams(dimension_semantics=("arbitrary",)),
        name="retention",
    )(g_len, state0, rq, rk, rv, decay, q_dec, k_dec)


def _top16_rows(x, ids=None, sentinel=None):
    if ids is None:
        ids = lax.broadcasted_iota(jnp.int32, x.shape, 0)
        sentinel = x.shape[0]
    vals, idxs = [], []
    for _ in range(PEER_TOPK):
        m = jnp.max(x, axis=0, keepdims=True)
        idx = jnp.min(jnp.where(x == m, ids, sentinel), axis=0, keepdims=True)
        vals.append(m)
        idxs.append(idx)
        x = jnp.where(ids == idx, -jnp.inf, x)
    return vals, idxs


def _top16_of_128(x):
    n_slab = N_KEYS // SUBLANES
    cols = x.shape[1]
    sub = lax.broadcasted_iota(jnp.int32, (SUBLANES, cols), 0)
    vals = [x[v * SUBLANES:(v + 1) * SUBLANES] for v in range(n_slab)]
    rows = [sub + v * SUBLANES for v in range(n_slab)]
    for p in range(n_slab):
        for a in range(p % 2, n_slab - 1, 2):
            lt = vals[a] < vals[a + 1]
            vals[a], vals[a + 1] = jnp.where(lt, vals[a + 1], vals[a]), jnp.where(lt, vals[a], vals[a + 1])
            rows[a], rows[a + 1] = jnp.where(lt, rows[a + 1], rows[a]), jnp.where(lt, rows[a], rows[a + 1])
    out_v, out_i = [], []
    for r in range(PEER_TOPK):
        m = jnp.max(vals[0], axis=0, keepdims=True)
        idx = jnp.min(jnp.where(vals[0] == m, rows[0], N_KEYS), axis=0, keepdims=True)
        out_v.append(m)
        out_i.append(idx)
        pop = rows[0] == idx
        for q in range(PEER_TOPK - 1 - r):
            vals[q] = jnp.where(pop, vals[q + 1], vals[q])
            rows[q] = jnp.where(pop, rows[q + 1], rows[q])
    return out_v, out_i


def _candidate_pairs():
    return [(a, b) for a in range(PEER_TOPK) for b in range(PEER_TOPK) if (a + 1) * (b + 1) <= PEER_TOPK]


N_CAND = 56


def _mix_route_kernel(x_ref, oatt_ref, oret_ref, rg_ref, lnin_g_ref, lnin_b_ref, gn_ref, wout_ref,
                      ln1_g_ref, ln1_b_ref, wq_ref, sk_ref, cid_ref,
                      h1_ref, eidx_ref, gate_ref, qs_ref):
    h = _layer_norm(x_ref[...], lnin_g_ref[...], lnin_b_ref[...])
    oret = oret_ref[...]
    rg = rg_ref[...]
    gn = gn_ref[...]
    ys = []
    for hh in range(RET_HEADS):
        sl = slice(hh * RET_DK, (hh + 1) * RET_DK)
        seg = oret[:, sl]
        mu = jnp.mean(seg, axis=-1, keepdims=True)
        sc = seg - mu
        var = jnp.mean(sc * sc, axis=-1, keepdims=True)
        gate = rg[:, sl]
        ys.append(sc * lax.rsqrt(var + LN_EPS) * gn[:, sl] * (gate * jax.nn.sigmoid(gate)))
    mixed = jnp.concatenate([oatt_ref[...]] + ys, axis=-1).astype(_BF16)
    mix = jnp.dot(mixed, wout_ref[...], preferred_element_type=_F32)
    h1 = _layer_norm(ALPHA * h + mix, ln1_g_ref[...], ln1_b_ref[...])
    h1_ref[...] = h1

    qp = jnp.dot(h1.astype(_BF16), wq_ref[...], preferred_element_type=_F32).astype(_BF16)
    for p in range(PEER_HEADS):
        qs_ref[p] = qp[:, p * LANES:(p + 1) * LANES]

    def head_pair(pp, carry):
        for i in range(ROUTE_HEADS_PER_ITER):
            head_body(pp + i * (PEER_HEADS // ROUTE_HEADS_PER_ITER))
        return carry

    def head_body(p):
        sc_t = lax.dot_general(sk_ref[p], qs_ref[p], (((1,), (1,)), ((), ())),
                               preferred_element_type=_F32)
        s1, i1 = _top16_of_128(sc_t[:N_KEYS])
        s2, i2 = _top16_of_128(sc_t[N_KEYS:])
        pairs = _candidate_pairs()
        pad = [jnp.full_like(s1[0], -jnp.inf)] * (N_CAND - len(pairs))
        cand = jnp.concatenate([s1[a] + s2[b] for a, b in pairs] + pad, axis=0)
        i1s = [i * N_KEYS for i in i1]
        cand_e = jnp.concatenate([i1s[a] + i2[b] for a, b in pairs] + [jnp.zeros_like(i1[0])] * len(pad),
                                 axis=0)
        cid = cid_ref[...]
        top, sel = _top16_rows(cand, cid, PEER_TOPK * PEER_TOPK)
        e_rows = [jnp.sum(jnp.where(cid == sel[r], cand_e, 0), axis=0, keepdims=True)
                  for r in range(PEER_TOPK)]
        topc = jnp.concatenate(top, axis=0)
        ex = jnp.exp(topc - top[0])
        gate = ex / jnp.sum(ex, axis=0, keepdims=True)
        off = pl.multiple_of(p * PEER_TOPK, PEER_TOPK)
        eidx_ref[pl.ds(off, PEER_TOPK), :] = jnp.concatenate(e_rows, axis=0)
        gate_ref[pl.ds(off, PEER_TOPK), :] = gate

    lax.fori_loop(0, PEER_HEADS // ROUTE_HEADS_PER_ITER, head_pair, 0)


def _mix_route(x2d, o_att, o_ret, rg, lnin_g, lnin_b, gn_g, w_out_bf, ln1_g, ln1_b, wq_bf, sk_blk):
    t = x2d.shape[0]
    tm = ROUTE_TM
    row = lambda i: (i, 0)
    const = lambda i: (0, 0)
    vec = pl.BlockSpec((1, D_MODEL), const)
    assert t % tm == 0
    flat = [a * PEER_TOPK + b for a, b in _candidate_pairs()]
    flat += [PEER_TOPK * PEER_TOPK] * (N_CAND - len(flat))
    cand_ids = jnp.asarray(np.broadcast_to(np.asarray(flat, np.int32)[:, None], (N_CAND, tm)))
    return pl.pallas_call(
        _mix_route_kernel,
        grid=(t // tm,),
        in_specs=[
            pl.BlockSpec((tm, D_MODEL), row),
            pl.BlockSpec((tm, ATT_W), row),
            pl.BlockSpec((tm, RET_W), row),
            pl.BlockSpec((tm, RET_W), row),
            vec, vec,
            pl.BlockSpec((1, RET_W), const),
            pl.BlockSpec((D_MODEL, D_MODEL), const),
            vec, vec,
            pl.BlockSpec((D_MODEL, D_MODEL), const),
            pl.BlockSpec((PEER_HEADS, 2 * N_KEYS, LANES), lambda i: (0, 0, 0)),
            pl.BlockSpec((N_CAND, tm), const),
        ],
        out_specs=[
            pl.BlockSpec((tm, D_MODEL), row),
            pl.BlockSpec((N_SEL, tm), lambda i: (0, i)),
            pl.BlockSpec((N_SEL, tm), lambda i: (0, i)),
        ],
        out_shape=[
            jax.ShapeDtypeStruct((t, D_MODEL), _F32),
            jax.ShapeDtypeStruct((N_SEL, t), jnp.int32),
            jax.ShapeDtypeStruct((N_SEL, t), _F32),
        ],
        scratch_shapes=[pltpu.VMEM((PEER_HEADS, tm, LANES), _BF16)],
        compiler_params=_vmem_limit(40 * 2 ** 20),
        name="mix_route",
    )(x2d, o_att, o_ret, rg, lnin_g, lnin_b, gn_g, w_out_bf, ln1_g, ln1_b, wq_bf, sk_blk, cand_ids)


PACK_ROWS = 512


def _pack_uv_kernel(u_ref, v_ref, o_ref, stage_ref):
    def group(g, carry):
        r0 = pl.multiple_of(g * SUBLANES, SUBLANES)
        for t, src in enumerate((u_ref, v_ref)):
            blk = src[pl.ds(r0, SUBLANES), :]
            for s in range(SUBLANES):
                stage_ref[t, s * SUBLANES:(s + 1) * SUBLANES, :] = blk[:, s * LANES:(s + 1) * LANES]
            for r in range(SUBLANES):
                o_ref[r0 + r, t] = stage_ref[t, pl.ds(r, SUBLANES, stride=SUBLANES), :]
        return carry

    lax.fori_loop(0, u_ref.shape[0] // SUBLANES, group, 0)


def _pack_uv(u_tab, v_tab):
    n = u_tab.shape[0]
    assert n % PACK_ROWS == 0
    row = lambda i: (i, 0)
    return pl.pallas_call(
        _pack_uv_kernel,
        grid=(n // PACK_ROWS,),
        in_specs=[pl.BlockSpec((PACK_ROWS, D_MODEL), row), pl.BlockSpec((PACK_ROWS, D_MODEL), row)],
        out_specs=pl.BlockSpec((PACK_ROWS, 2) + ROW_TILE, lambda i: (i, 0, 0, 0)),
        out_shape=jax.ShapeDtypeStruct((n, 2) + ROW_TILE, _F32),
        scratch_shapes=[pltpu.VMEM((2, SUBLANES * SUBLANES, LANES), _F32)],
        name="pack_uv",
    )(u_tab, v_tab)


_BITREV8 = (0, 4, 2, 6, 1, 5, 3, 7)


def _rows_to_sublanes(tiles, sub):
    m4 = sub < 4
    m2 = (sub & 2) == 0
    m1 = (sub & 1) == 0
    t = [tiles[i] for i in _BITREV8]
    c = []
    for a, b in ((0, 1), (2, 3), (4, 5), (6, 7)):
        w = jnp.where(m4, t[a], t[b])
        x = jnp.where(m4, t[b], t[a])
        c.append(w + pltpu.roll(x, 4, axis=0))
    d = []
    for a, b in ((0, 1), (2, 3)):
        d.append(jnp.where(m2, c[a] + pltpu.roll(c[a], 6, axis=0), c[b] + pltpu.roll(c[b], 2, axis=0)))
    return jnp.where(m1, d[0] + pltpu.roll(d[0], 7, axis=0), d[1] + pltpu.roll(d[1], 1, axis=0))


def _peer_apply_kernel(idx_hbm, h1_ref, gate_ref, ln2_g_ref, ln2_b_ref, uv_hbm,
                       y_ref, uvbuf, idx_smem, gsem, isem, xs_ref, wb_ref, o_buf, *, n_tiles):
    tb, nt = PEER_TB, PEER_NT
    step_tok = tb * nt
    n_steps = n_tiles // nt
    j = pl.program_id(0)
    issue_on = j < n_steps
    comp_on = j >= 1
    lane0 = ((j - 1) * step_tok) % LANES

    def idx_copy(tile, buf):
        return pltpu.make_async_copy(idx_hbm.at[tile], idx_smem.at[buf], isem.at[buf])

    @pl.when(j == 0)
    def _():
        for i in range(nt):
            idx_copy(i, i).start()

    @pl.when(comp_on)
    def _():
        h1 = h1_ref[...]
        for s in range(SUBLANES):
            xs_ref[s * step_tok:(s + 1) * step_tok, :] = h1[:, s * LANES:(s + 1) * LANES]

    def finish_rows(i):
        r0 = i * tb
        out = jnp.concatenate([o_buf[pl.ds(r0 * SUBLANES + s, tb, stride=SUBLANES), :]
                               for s in range(SUBLANES)], axis=-1)
        y_ref[r0:r0 + tb, :] = _layer_norm(ALPHA * h1_ref[r0:r0 + tb, :] + out,
                                           ln2_g_ref[...], ln2_b_ref[...])

    def tile_pass(i, do_issue, do_comp):
        n_groups = N_SEL // SUBLANES
        tile_new = nt * j + i
        islot = tile_new % PEER_SLOTS
        cslot = (tile_new - nt) % PEER_SLOTS
        if do_issue:
            idx_copy(tile_new, i).wait()
        if do_comp:
            pltpu.make_async_copy(uv_hbm.at[pl.ds(0, tb * N_SEL)], uvbuf.at[cslot], gsem.at[cslot]).wait()

        def make_issue(tok):
            pending = list(range(N_SEL)) if do_issue else []

            def issue(n):
                for _ in range(min(n, len(pending))):
                    k = pending.pop(0)
                    e = idx_smem[i, tok, k]
                    pltpu.make_async_copy(uv_hbm.at[e], uvbuf.at[islot, tok * N_SEL + k],
                                          gsem.at[islot]).start(priority=k % 2)
            return issue

        def refill():
            @pl.when(tile_new + nt < n_tiles)
            def _():
                idx_copy(tile_new + nt, i).start()

        if not do_comp:
            def issue_body(tok, carry):
                make_issue(tok)(N_SEL)
                return carry
            lax.fori_loop(0, tb, issue_body, 0)
            refill()
            return

        issues = [make_issue(tok) for tok in range(tb)]
        sub = lax.broadcasted_iota(jnp.int32, ROW_TILE, 0)
        if i > 0:
            finish_rows(i - 1)

        lane = lax.broadcasted_iota(jnp.int32, (N_SEL, LANES), 1)
        act = jnp.zeros((N_SEL, LANES), _F32)
        for tok in range(tb):
            xt = xs_ref[pl.ds(i * tb + tok, SUBLANES, stride=step_tok), :]
            parts = []

            def load_u(g):
                return [uvbuf[cslot, tok * N_SEL + g * SUBLANES + r, 0] for r in range(SUBLANES)]

            for g0 in range(0, n_groups, PEER_REGION):
                region = [load_u(g) for g in range(g0, g0 + PEER_REGION)]
                issues[tok](PEER_ISSUE_U * PEER_REGION)
                for rows in region:
                    prods = [row * xt for row in rows]
                    parts.append(jnp.sum(_rows_to_sublanes(prods, sub), axis=1, keepdims=True))
            act = jnp.where(lane == tok, jnp.concatenate(parts, axis=0), act)

        gate0 = pltpu.roll(gate_ref[...], (2 * LANES - lane0 - i * tb) % LANES, axis=1)
        w = jax.nn.gelu(act) * gate0
        for tok in range(tb):
            wb_ref[:, tok * LANES:(tok + 1) * LANES] = jnp.broadcast_to(w[:, tok:tok + 1], (N_SEL, LANES))

        for tok in range(tb):
            accs = [jnp.zeros(ROW_TILE, _F32) for _ in range(4)]

            def load_v(k0):
                ks = range(k0, k0 + SUBLANES)
                return ([jnp.broadcast_to(wb_ref[k:k + 1, tok * LANES:(tok + 1) * LANES], ROW_TILE) for k in ks],
                        [uvbuf[cslot, tok * N_SEL + k, 1] for k in ks])

            for k0 in range(0, N_SEL, 2 * SUBLANES):
                pair = [load_v(k0), load_v(k0 + SUBLANES)]
                issues[tok](2 * (N_SEL // SUBLANES - PEER_ISSUE_U))
                for wks, vks in pair:
                    for r, (wk, vk) in enumerate(zip(wks, vks)):
                        accs[r % 4] = accs[r % 4] + wk * vk
            issues[tok](N_SEL)
            ltok = i * tb + tok
            o_buf[ltok * SUBLANES:(ltok + 1) * SUBLANES, :] = (accs[0] + accs[1]) + (accs[2] + accs[3])
        if do_issue:
            refill()

    for i in range(nt):
        pl.when(jnp.logical_and(issue_on, jnp.logical_not(comp_on)))(functools.partial(tile_pass, i, True, False))
        pl.when(jnp.logical_and(issue_on, comp_on))(functools.partial(tile_pass, i, True, True))
        pl.when(jnp.logical_and(jnp.logical_not(issue_on), comp_on))(functools.partial(tile_pass, i, False, True))

    @pl.when(comp_on)
    def _():
        finish_rows(nt - 1)


def _peer_apply(h1, eidx_t, gate_t, uv_tab, ln2_g, ln2_b):
    t = h1.shape[0]
    tb, nt = PEER_TB, PEER_NT
    step_tok = tb * nt
    assert t % step_tok == 0 and LANES % step_tok == 0 and PEER_SLOTS > nt
    n_tiles = t // tb
    idx_tiles = eidx_t.T.reshape(n_tiles, tb, N_SEL)
    done = lambda j: jnp.maximum(j - 1, 0)
    const = lambda j: (0, 0)
    kern = functools.partial(_peer_apply_kernel, n_tiles=n_tiles)
    return pl.pallas_call(
        kern,
        grid=(n_tiles // nt + 1,),
        in_specs=[
            pl.BlockSpec(memory_space=pl.ANY),
            pl.BlockSpec((step_tok, D_MODEL), lambda j: (done(j), 0)),
            pl.BlockSpec((N_SEL, LANES), lambda j: (0, done(j) * step_tok // LANES)),
            pl.BlockSpec((1, D_MODEL), const),
            pl.BlockSpec((1, D_MODEL), const),
            pl.BlockSpec(memory_space=pl.ANY),
        ],
        out_specs=pl.BlockSpec((step_tok, D_MODEL), lambda j: (done(j), 0)),
        out_shape=jax.ShapeDtypeStruct((t, D_MODEL), _F32),
        scratch_shapes=[
            pltpu.VMEM((PEER_SLOTS, tb * N_SEL, 2) + ROW_TILE, _F32),
            pltpu.SMEM((nt, tb, N_SEL), jnp.int32),
            pltpu.SemaphoreType.DMA((PEER_SLOTS,)),
            pltpu.SemaphoreType.DMA((nt,)),
            pltpu.VMEM((SUBLANES * step_tok, LANES), _F32),
            pltpu.VMEM((N_SEL, tb * LANES), _F32),
            pltpu.VMEM((step_tok * SUBLANES, LANES), _F32),
        ],
        compiler_params=pltpu.CompilerParams(
            dimension_semantics=("arbitrary",),
            vmem_limit_bytes=int(PEER_SLOTS * tb * N_SEL * 2 * 4096 + 12 * 2 ** 20)),
        name="peer_apply",
    )(idx_tiles, h1, gate_t, ln2_g, ln2_b, uv_tab)


def _rope_tables(pos):
    half = RET_DK // 2
    inv = ROPE_BASE ** (-jnp.arange(half, dtype=_F32) / half)
    ang = jnp.asarray(pos, _F32)[:, None] * inv[None, :]
    cos = jnp.cos(ang)
    sin = jnp.sin(ang)
    return jnp.concatenate([cos, cos], axis=-1), jnp.concatenate([-sin, sin], axis=-1)


def _subkey_blocks(sub_keys):
    z = jnp.zeros((PEER_HEADS, N_KEYS, PEER_DK_HALF), sub_keys.dtype)
    top = jnp.concatenate([sub_keys[:, 0], z], axis=-1)
    bot = jnp.concatenate([z, sub_keys[:, 1]], axis=-1)
    return jnp.concatenate([top, bot], axis=1).astype(_BF16)


def kernel(x_prompt, x_sample, cache_meta_k, cache_meta_v, cache_swa_k, cache_swa_v, state_ret,
           meta_tokens, ln_in_g, ln_in_b, rel_bias, w_in, w_out, attn_sinks, ret_gn_g, ln1_g, ln1_b,
           peer_wq, peer_subkeys, peer_u, peer_v, ln2_g, ln2_b):
    b, s, _ = x_prompt.shape
    db, ds, _ = x_sample.shape
    row = lambda a: a.reshape(1, -1)

    w_in_bf = w_in[0].astype(_BF16)
    w_out_bf = w_out[0].astype(_BF16)
    wq_bf = peer_wq[0].astype(_BF16)
    sk_blk = _subkey_blocks(peer_subkeys[0])
    uv_tab = _pack_uv(peer_u[0], peer_v[0])
    lnin_g, lnin_b = row(ln_in_g), row(ln_in_b)
    cos_m, sin_m = _rope_tables(np.arange(N_META))
    cos_p, sin_p = _rope_tables(N_META + np.arange(s))
    cos_s, sin_s = _rope_tables(N_META + PAST_LEN + np.arange(ds))
    bias_tab = _rel_bias_tables(rel_bias)
    sinks = attn_sinks[0]

    _, kv_m, rq_m, rk_m, rv_m, _ = _ln_proj(meta_tokens.astype(x_prompt.dtype), N_META, lnin_g, lnin_b,
                                         w_in_bf, cos_m, sin_m, N_META)
    xp2 = x_prompt.reshape(b * s, D_MODEL)
    xs2 = x_sample.reshape(db * ds, D_MODEL)
    q_p, kv_p, rq_p, rk_p, rv_p, rg_p = _ln_proj(xp2, s, lnin_g, lnin_b, w_in_bf, cos_p, sin_p, PROJ_TM)
    q_s, kv_s, rq_s, rk_s, rv_s, rg_s = _ln_proj(xs2, ds, lnin_g, lnin_b, w_in_bf, cos_s, sin_s, ds)

    kv_p3 = kv_p.reshape(b, s, 2 * KV_W)
    o_att_p = _attention_prompt(q_p.reshape(b, s, ATT_W), kv_p3, kv_m, bias_tab, sinks)
    kv_s3 = kv_s.reshape(db, ds, 2 * KV_W)
    o_att_s = _attention_sample(
        q_s.reshape(db, ds, ATT_W), kv_s3,
        cache_swa_k[0].reshape(db, WINDOW, KV_W), cache_swa_v[0].reshape(db, WINDOW, KV_W),
        cache_meta_k[0].reshape(db, N_META, KV_W), cache_meta_v[0].reshape(db, N_META, KV_W),
        bias_tab, sinks)

    zero_state = jnp.zeros((1, RET_HEADS, RET_DK, RET_DK), _F32)
    _, st_meta = _retention(zero_state, rq_m[None], rk_m[None], rv_m[None], N_META)
    st0_p = jnp.broadcast_to(st_meta, (b, RET_HEADS, RET_DK, RET_DK))
    seq3 = lambda a, n, l: a.reshape(n, l, RET_W)
    o_ret_p, st_p = _retention(st0_p, seq3(rq_p, b, s), seq3(rk_p, b, s), seq3(rv_p, b, s), CHUNK)
    o_ret_s, st_s = _retention(state_ret[0].astype(_F32), seq3(rq_s, db, ds), seq3(rk_s, db, ds),
                               seq3(rv_s, db, ds), ds)

    def tail(x2d, o_att, o_ret, rg):
        h1, eidx_t, gate_t = _mix_route(x2d, o_att, o_ret, rg, lnin_g, lnin_b, row(ret_gn_g[0]),
                                        w_out_bf, row(ln1_g[0]), row(ln1_b[0]), wq_bf, sk_blk)
        return _peer_apply(h1, eidx_t, gate_t, uv_tab, row(ln2_g[0]), row(ln2_b[0]))

    y_p = tail(xp2, o_att_p.reshape(b * s, ATT_W), o_ret_p.reshape(b * s, RET_W), rg_p)
    y_s = tail(xs2, o_att_s.reshape(db * ds, ATT_W), o_ret_s.reshape(db * ds, RET_W), rg_s)

    kvh = lambda a, n, l: a.reshape(1, n, l, KV_HEADS, HEAD_DIM)
    meta_k = jnp.broadcast_to(kvh(kv_m[:, :KV_W], 1, N_META), (1, b, N_META, KV_HEADS, HEAD_DIM))
    meta_v = jnp.broadcast_to(kvh(kv_m[:, KV_W:], 1, N_META), (1, b, N_META, KV_HEADS, HEAD_DIM))
    tail_kv = kv_p3[:, s - WINDOW:]
    return (y_p.reshape(b, s, D_MODEL), y_s.reshape(db, ds, D_MODEL),
            meta_k, meta_v,
            kvh(tail_kv[..., :KV_W], b, WINDOW), kvh(tail_kv[..., KV_W:], b, WINDOW),
            st_p[None],
            kvh(kv_s3[..., :KV_W], db, ds), kvh(kv_s3[..., KV_W:], db, ds),
            st_s[None])
```
